```python
import math
import jax, jax.numpy as jnp
from jax import lax
import numpy as np

D_MODEL = 1024
BATCH = 8
SEQ = 2048
DEPTH = 1

D_MIX = D_MODEL
ATTN_HEADS = 8
HEAD_DIM = 64
D_ATTN = ATTN_HEADS * HEAD_DIM
SGU_GROUPS = 4
SGU_CH = 128
D_SGU = SGU_GROUPS * SGU_CH
D_IN_PROJ = 3 * D_ATTN + 2 * D_SGU
CHUNK = 128
DILATED = ((128, 1), (512, 4), (2048, 16))
BLK = 128
NUM_BUCKETS = 32
MAX_DISTANCE = 2048
PEER_HEADS = 8
PEER_KEYS = 128
PEER_EXPERTS = PEER_KEYS * PEER_KEYS
PEER_D_KEY = 256
PEER_TOPK = 16
PEER_CHUNK = 128
EPS = 1e-6
NEG = -1e30

kernel_name = "hymba_sgu_dilated_attn_peer_layer"


def _rms(x, w):
    xf = x.astype(jnp.float32)
    y = xf * lax.rsqrt(jnp.mean(xf * xf, axis=-1, keepdims=True) + EPS)
    return (y * w.astype(jnp.float32)).astype(x.dtype)


def _t5_bucket(dist):
    max_exact = NUM_BUCKETS // 2
    d_f = jnp.maximum(dist, max_exact).astype(jnp.float32)
    large = max_exact + (jnp.log(d_f / max_exact) / math.log(MAX_DISTANCE / max_exact)
                         * (NUM_BUCKETS - max_exact)).astype(jnp.int32)
    large = jnp.minimum(large, NUM_BUCKETS - 1)
    return jnp.where(dist < max_exact, dist, large)


def _dilated_window_attention(q, k, v, rel_bias, window, dil):
    b, s, h, dh = q.shape
    w_sub = window // dil
    assert w_sub <= BLK
    L = s // dil
    nblk = -(-L // BLK)
    Lp = nblk * BLK

    def to_res(t):
        return t.reshape(b, L, dil, h, dh).transpose(0, 2, 1, 3, 4).reshape(b * dil, L, h, dh)

    qr, kr, vr = to_res(q), to_res(k), to_res(v)
    qb = jnp.pad(qr, ((0, 0), (0, Lp - L), (0, 0), (0, 0))).reshape(b * dil, nblk, BLK, h, dh)

    def kv_blocks(t):
        tp = jnp.pad(t, ((0, 0), (BLK, Lp - L), (0, 0), (0, 0))).reshape(b * dil, nblk + 1, BLK, h, dh)
        return jnp.concatenate([tp[:, :-1], tp[:, 1:]], axis=2)

    kb, vb = kv_blocks(kr), kv_blocks(vr)
    logits = jnp.einsum('bnqhd,bnkhd->bnhqk', qb, kb).astype(jnp.float32) * (HEAD_DIM ** -0.5)
    qi = jnp.arange(BLK)[:, None]
    kj = jnp.arange(2 * BLK)[None, :]
    rel = BLK + qi - kj
    bias = rel_bias[_t5_bucket(jnp.maximum(rel, 0) * dil)]
    logits = logits + bias.transpose(2, 0, 1).astype(jnp.float32)[None, None]
    kpos = jnp.arange(nblk)[:, None, None] * BLK - BLK + kj[None]
    valid = (rel >= 0)[None] & (rel <= w_sub)[None] & (kpos >= 0)
    logits = jnp.where(valid[None, :, None], logits, NEG)
    m = jnp.max(logits, axis=-1, keepdims=True)
    p = jnp.exp(logits - m)
    denom = jnp.sum(p, axis=-1, keepdims=True)
    o = jnp.einsum('bnhqk,bnkhd->bnqhd', p, vb.astype(jnp.float32)) / denom.transpose(0, 1, 3, 2, 4)
    lse = (m + jnp.log(denom))[..., 0].transpose(0, 1, 3, 2)
    o = o.reshape(b * dil, Lp, h, dh)[:, :L].reshape(b, dil, L, h, dh).transpose(0, 2, 1, 3, 4).reshape(b, s, h, dh)
    lse = lse.reshape(b * dil, Lp, h)[:, :L].reshape(b, dil, L, h).transpose(0, 2, 1, 3).reshape(b, s, h)
    return o, lse


def _spatial_gating(u, va, norm_w, w_s, b_s):
    b, s, _ = u.shape
    nc = s // CHUNK
    u = u.reshape(b, nc, CHUNK, SGU_GROUPS, SGU_CH)
    va = _rms(va.reshape(b, nc, CHUNK, SGU_GROUPS, SGU_CH), norm_w)
    w_masked = w_s * jnp.tril(jnp.ones((CHUNK, CHUNK), w_s.dtype))[None]
    spatial = jnp.einsum('gts,bnsgc->bntgc', w_masked, va) + b_s.T[None, None, :, :, None]
    return (u * spatial).reshape(b, s, D_SGU)


def _peer(xn, w_query, keys1, keys2, u_tab, v_tab):
    b, s, d = xn.shape
    t = b * s
    half = PEER_D_KEY // 2
    xf = xn.reshape(t, d)
    q = (xf @ w_query).reshape(t, PEER_HEADS, PEER_D_KEY)
    s1 = jnp.einsum('thd,hkd->thk', q[..., :half], keys1).astype(jnp.float32)
    s2 = jnp.einsum('thd,hkd->thk', q[..., half:], keys2).astype(jnp.float32)
    v1, i1 = lax.top_k(s1, PEER_TOPK)
    v2, i2 = lax.top_k(s2, PEER_TOPK)
    cand = (v1[..., :, None] + v2[..., None, :]).reshape(t, PEER_HEADS, PEER_TOPK * PEER_TOPK)
    cidx = (i1[..., :, None] * PEER_KEYS + i2[..., None, :]).reshape(t, PEER_HEADS, PEER_TOPK * PEER_TOPK)
    best, pos = lax.top_k(cand, PEER_TOPK)
    idx = jnp.take_along_axis(cidx, pos, axis=-1)
    gate = jax.nn.softmax(best, axis=-1)
    n_chunks = t // PEER_CHUNK
    n_sel = PEER_HEADS * PEER_TOPK

    def expert_block(args):
        xc, ic, gc = args
        u = jnp.take(u_tab, ic, axis=0)
        act = jax.nn.gelu(jnp.einsum('cd,ced->ce', xc, u).astype(jnp.float32))
        vv = jnp.take(v_tab, ic, axis=0)
        return jnp.einsum('ce,ced->cd', gc * act, vv.astype(jnp.float32)).astype(xc.dtype)

    y = lax.map(expert_block, (xf.reshape(n_chunks, PEER_CHUNK, d),
                               idx.reshape(n_chunks, PEER_CHUNK, n_sel),
                               gate.reshape(n_chunks, PEER_CHUNK, n_sel)))
    return y.reshape(b, s, d)


def setup_inputs(seed: int = 0) -> dict:
    key = jax.random.key(seed)
    ks = jax.random.split(key, 20)
    f32 = jnp.float32
    nrm = lambda k, shape, scale: jax.random.normal(k, shape, f32) * scale
    gain = lambda k, shape: 1.0 + 0.05 * jax.random.normal(k, shape, f32)
    return {
        "x": nrm(ks[0], (BATCH, SEQ, D_MODEL), 1.0),
        "ln1_w": gain(ks[1], (DEPTH, D_MODEL)),
        "w_in": nrm(ks[2], (DEPTH, D_MODEL, D_IN_PROJ), D_MODEL ** -0.5),
        "q_norm_w": gain(ks[3], (DEPTH, HEAD_DIM)),
        "k_norm_w": gain(ks[4], (DEPTH, HEAD_DIM)),
        "sgu_norm_w": gain(ks[5], (DEPTH, SGU_GROUPS, SGU_CH)),
        "sgu_w": nrm(ks[6], (DEPTH, SGU_GROUPS, CHUNK, CHUNK), CHUNK ** -0.5),
        "sgu_b": gain(ks[7], (DEPTH, SGU_GROUPS, CHUNK)),
        "mix_norm_w": gain(ks[8], (DEPTH, D_MIX)),
        "w_out": nrm(ks[9], (DEPTH, D_MIX, D_MODEL), D_MIX ** -0.5),
        "ln2_w": gain(ks[10], (DEPTH, D_MODEL)),
        "peer_w_query": nrm(ks[11], (DEPTH, D_MODEL, PEER_HEADS * PEER_D_KEY), D_MODEL ** -0.5),
        "peer_keys1": nrm(ks[12], (DEPTH, PEER_HEADS, PEER_KEYS, PEER_D_KEY // 2), (PEER_D_KEY // 2) ** -0.5),
        "peer_keys2": nrm(ks[13], (DEPTH, PEER_HEADS, PEER_KEYS, PEER_D_KEY // 2), (PEER_D_KEY // 2) ** -0.5),
        "peer_u": nrm(ks[14], (DEPTH, PEER_EXPERTS, D_MODEL), D_MODEL ** -0.5),
        "peer_v": nrm(ks[15], (DEPTH, PEER_EXPERTS, D_MODEL), D_MODEL ** -0.5),
        "rel_bias": nrm(ks[16], (NUM_BUCKETS, ATTN_HEADS), 0.1),
    }


def reference(x, ln1_w, w_in, q_norm_w, k_norm_w, sgu_norm_w, sgu_w, sgu_b, mix_norm_w, w_out,
              ln2_w, peer_w_query, peer_keys1, peer_keys2, peer_u, peer_v, rel_bias):
    b, s, _ = x.shape
    for l in range(DEPTH):
        h = _rms(x, ln1_w[l])
        proj = h @ w_in[l]
        q = proj[..., 0:D_ATTN].reshape(b, s, ATTN_HEADS, HEAD_DIM)
        k = proj[..., D_ATTN:2 * D_ATTN].reshape(b, s, ATTN_HEADS, HEAD_DIM)
        v = proj[..., 2 * D_ATTN:3 * D_ATTN].reshape(b, s, ATTN_HEADS, HEAD_DIM)
        ua = jax.nn.gelu(proj[..., 3 * D_ATTN:3 * D_ATTN + D_SGU])
        va = jax.nn.gelu(proj[..., 3 * D_ATTN + D_SGU:])
        q = _rms(q, q_norm_w[l])
        k = _rms(k, k_norm_w[l])
        outs, lses = [], []
        for window, dil in DILATED:
            o_p, lse_p = _dilated_window_attention(q, k, v, rel_bias, window, dil)
            outs.append(o_p)
            lses.append(lse_p)
        wts = jax.nn.softmax(jnp.stack(lses, axis=0), axis=0)
        attn = jnp.sum(wts[..., None] * jnp.stack(outs, axis=0), axis=0)
        attn = attn.astype(x.dtype).reshape(b, s, D_ATTN)
        sgu = _spatial_gating(ua, va, sgu_norm_w[l], sgu_w[l], sgu_b[l])
        mix = jnp.concatenate([_rms(attn, mix_norm_w[l, :D_ATTN]),
                               _rms(sgu, mix_norm_w[l, D_ATTN:])], axis=-1)
        x = x + mix @ w_out[l]
        hn = _rms(x, ln2_w[l])
        x = x + _peer(hn, peer_w_query[l], peer_keys1[l], peer_keys2[l], peer_u[l], peer_v[l])
    return x
```

```python
import functools
import math

import numpy as np
import jax
import jax.numpy as jnp
from jax import lax
from jax.experimental import pallas as pl
from jax.experimental.pallas import tpu as pltpu

F32 = jnp.float32
BF16 = jnp.bfloat16

ATTN_HEADS = 8
HEAD_DIM = 64
D_ATTN = ATTN_HEADS * HEAD_DIM
SGU_GROUPS = 4
SGU_CH = 128
D_SGU = SGU_GROUPS * SGU_CH
CHUNK = 128
DILATED = ((128, 1), (512, 4), (2048, 16))
BLK = 128
NUM_BUCKETS = 32
MAX_DISTANCE = 2048
PEER_HEADS = 8
PEER_KEYS = 128
PEER_D_KEY = 256
PEER_TOPK = 16
EPS = 1e-6
NEG = -1e30

LANES = 128
VMEM_LIMIT = 56 * 1024 * 1024


def _rms_rows(x, w):
    return x * lax.rsqrt(jnp.mean(x * x, axis=-1, keepdims=True) + EPS) * w


def _bucket_tables():
    qi = np.arange(BLK)[:, None]
    kj = np.arange(2 * BLK)[None, :]
    rel = BLK + qi - kj
    max_exact = NUM_BUCKETS // 2
    buckets, valid = [], []
    for window, dil in DILATED:
        w_sub = window // dil
        dist = np.maximum(rel, 0) * dil
        d_f = np.maximum(dist, max_exact).astype(np.float32)
        large = max_exact + (np.log(d_f / np.float32(max_exact)) / np.float32(math.log(MAX_DISTANCE / max_exact))
                             * np.float32(NUM_BUCKETS - max_exact)).astype(np.int32)
        large = np.minimum(large, NUM_BUCKETS - 1)
        buckets.append(np.where(dist < max_exact, dist, large).astype(np.int32))
        valid.append(((rel >= 0) & (rel <= w_sub)).astype(np.int32))
    return np.stack(buckets), np.stack(valid)


def _bias_kernel(rb_ref, bkt_ref, valid_ref, o_ref):
    h = pl.program_id(1)
    bkt = bkt_ref[...]
    acc = jnp.zeros(bkt.shape, F32)
    for j in range(NUM_BUCKETS):
        acc = jnp.where(bkt == j, rb_ref[j, h], acc)
    o_ref[...] = jnp.where(valid_ref[...] > 0, acc, NEG)


def _bias_tables(rel_bias):
    bkt, valid = _bucket_tables()
    npat = len(DILATED)
    blk = pl.BlockSpec((None, BLK, 2 * BLK), lambda p, h: (p, 0, 0))
    return pl.pallas_call(
        _bias_kernel,
        grid=(npat, ATTN_HEADS),
        in_specs=[pl.BlockSpec(memory_space=pltpu.SMEM), blk, blk],
        out_specs=pl.BlockSpec((None, None, BLK, 2 * BLK), lambda p, h: (p, h, 0, 0)),
        out_shape=jax.ShapeDtypeStruct((npat, ATTN_HEADS, BLK, 2 * BLK), F32),
        name="bias",
    )(rel_bias, jnp.asarray(bkt), jnp.asarray(valid))


def _in_proj_kernel(x_ref, ln1_ref, win_ref, sgnw_ref, sguw_ref, bst_ref, mixw_ref,
                    qkv_ref, sgu_ref, sgu_s):
    tm = x_ref.shape[0]
    h = _rms_rows(x_ref[...], ln1_ref[...]).astype(BF16)
    proj = jnp.dot(h, win_ref[...], preferred_element_type=F32)
    qkv_ref[...] = proj[:, :3 * D_ATTN]
    row = lax.broadcasted_iota(jnp.int32, (CHUNK, CHUNK), 0)
    col = lax.broadcasted_iota(jnp.int32, (CHUNK, CHUNK), 1)
    for g in range(SGU_GROUPS):
        u_lo = 3 * D_ATTN + g * SGU_CH
        v_lo = 3 * D_ATTN + D_SGU + g * SGU_CH
        ua = jax.nn.gelu(proj[:, u_lo:u_lo + SGU_CH])
        va = jax.nn.gelu(proj[:, v_lo:v_lo + SGU_CH])
        vn = _rms_rows(va, sgnw_ref[g:g + 1, :]).astype(BF16)
        w = jnp.where(row >= col, sguw_ref[g], 0.0).astype(BF16)
        bias = bst_ref[:, g:g + 1]
        for c in range(tm // CHUNK):
            rows = slice(c * CHUNK, (c + 1) * CHUNK)
            spatial = jnp.dot(w, vn[rows], preferred_element_type=F32) + bias
            sgu_s[rows, g * SGU_CH:(g + 1) * SGU_CH] = ua[rows] * spatial
    sgu_ref[...] = _rms_rows(sgu_s[...], mixw_ref[...]).astype(BF16)


def _in_proj(x2, ln1, w_in, sgu_norm_w, sgu_w, sgu_b, mixw_sgu, tm):
    t, d = x2.shape
    dproj = w_in.shape[1]
    full = lambda shape: pl.BlockSpec(shape, lambda i: (0,) * len(shape))
    return pl.pallas_call(
        _in_proj_kernel,
        grid=(t // tm,),
        in_specs=[pl.BlockSpec((tm, d), lambda i: (i, 0)),
                  full((1, d)), full((d, dproj)), full((SGU_GROUPS, SGU_CH)),
                  full((SGU_GROUPS, CHUNK, CHUNK)), full((CHUNK, SGU_GROUPS)), full((1, D_SGU))],
        out_specs=[pl.BlockSpec((tm, 3 * D_ATTN), lambda i: (i, 0)),
                   pl.BlockSpec((tm, D_SGU), lambda i: (i, 0))],
        out_shape=[jax.ShapeDtypeStruct((t, 3 * D_ATTN), F32),
                   jax.ShapeDtypeStruct((t, D_SGU), BF16)],
        scratch_shapes=[pltpu.VMEM((tm, D_SGU), F32)],
        compiler_params=pltpu.CompilerParams(dimension_semantics=("arbitrary",),
                                             vmem_limit_bytes=VMEM_LIMIT),
        name="in_proj",
    )(x2, ln1, w_in, sgu_norm_w, sgu_w, sgu_b.T, mixw_sgu)


def _attn_kernel(q_ref, k_ref, v_ref, qw_ref, kw_ref, bias_ref, o_ref,
                 qs, ks, vs, acc_s, m_s, l_s):
    s_len = q_ref.shape[0]
    for hh in range(LANES // HEAD_DIM):
        lanes = slice(hh * HEAD_DIM, (hh + 1) * HEAD_DIM)
        qs[...] = _rms_rows(q_ref[:, lanes], qw_ref[...]) * (HEAD_DIM ** -0.5)
        ks[...] = _rms_rows(k_ref[:, lanes], kw_ref[...])
        vs[...] = v_ref[:, lanes]

        for p, (_, dil) in enumerate(DILATED):
            nblk = s_len // dil // BLK
            shift = dil.bit_length() - 1

            def block(qstart, kstart, nk, bias, p=p, dil=dil):
                qrows = pl.ds(qstart, BLK, stride=dil)
                krows = pl.ds(kstart, nk, stride=dil)
                qb = qs[qrows, :].astype(BF16)
                kb = ks[krows, :].astype(BF16)
                vb = vs[krows, :].astype(BF16)
                s = lax.dot_general(qb, kb, (((1,), (1,)), ((), ())), preferred_element_type=F32)
                logits = jnp.where(bias > 0.1 * NEG, s + bias, NEG)
                m = jnp.max(logits, axis=-1, keepdims=True)
                pe = jnp.exp(logits - m)
                acc_s[p, qrows, :] = jnp.dot(pe.astype(BF16), vb, preferred_element_type=F32)
                m_s[p, qrows, :] = m
                l_s[p, qrows, :] = jnp.sum(pe, axis=-1, keepdims=True)

            def first_block(r, carry, block=block, p=p, hh=hh):
                block(r, r, BLK, bias_ref[p, hh, :, BLK:])
                return carry

            def later_block(i, carry, block=block, p=p, hh=hh, dil=dil, shift=shift):
                r = jnp.bitwise_and(i, dil - 1)
                n = jnp.right_shift(i, shift)
                qstart = r + n * (dil * BLK)
                block(qstart, qstart - dil * BLK, 2 * BLK, bias_ref[p, hh])
                return carry

            lax.fori_loop(0, dil, first_block, 0)
            if nblk > 1:
                lax.fori_loop(dil, dil * nblk, later_block, 0)

        ms = [m_s[p] for p in range(len(DILATED))]
        m_all = functools.reduce(jnp.maximum, ms)
        es = [jnp.exp(m - m_all) for m in ms]
        num = sum(e * acc_s[p] for p, e in enumerate(es))
        den = sum(e * l_s[p] for p, e in enumerate(es))
        o_ref[:, lanes] = num / den


def _attention(qkv3, q_norm_w, k_norm_w, bias):
    b, s, _ = qkv3.shape
    nhp = D_ATTN // LANES
    npat = len(DILATED)
    col = lambda off: pl.BlockSpec((None, s, LANES), lambda bi, hp, off=off: (bi, 0, off + hp))
    return pl.pallas_call(
        _attn_kernel,
        grid=(b, nhp),
        in_specs=[col(0), col(nhp), col(2 * nhp),
                  pl.BlockSpec((1, HEAD_DIM), lambda bi, hp: (0, 0)),
                  pl.BlockSpec((1, HEAD_DIM), lambda bi, hp: (0, 0)),
                  pl.BlockSpec((npat, LANES // HEAD_DIM, BLK, 2 * BLK), lambda bi, hp: (0, hp, 0, 0))],
        out_specs=pl.BlockSpec((None, s, LANES), lambda bi, hp: (bi, 0, hp)),
        out_shape=jax.ShapeDtypeStruct((b, s, D_ATTN), F32),
        scratch_shapes=[pltpu.VMEM((s, HEAD_DIM), F32), pltpu.VMEM((s, HEAD_DIM), F32),
                        pltpu.VMEM((s, HEAD_DIM), F32),
                        pltpu.VMEM((npat, s, HEAD_DIM), F32),
                        pltpu.VMEM((npat, s, 1), F32), pltpu.VMEM((npat, s, 1), F32)],
        compiler_params=pltpu.CompilerParams(dimension_semantics=("arbitrary", "arbitrary"),
                                             vmem_limit_bytes=VMEM_LIMIT),
        name="attn",
    )(qkv3, qkv3, qkv3, q_norm_w, k_norm_w, bias)


def _out_proj_kernel(attn_ref, sgu_ref, x_ref, mixw_ref, wout_ref, ln2_ref, x1_ref, hnt_ref):
    an = _rms_rows(attn_ref[...], mixw_ref[...]).astype(BF16)
    y = jnp.dot(an, wout_ref[:D_ATTN, :], preferred_element_type=F32)
    y = y + jnp.dot(sgu_ref[...], wout_ref[D_ATTN:, :], preferred_element_type=F32)
    x1 = x_ref[...] + y
    x1_ref[...] = x1
    hnt_ref[...] = _rms_rows(x1, ln2_ref[...]).T.astype(BF16)


def _out_proj(attn2, sgu_n, x2, mixw_attn, w_out, ln2, tm):
    t, d = x2.shape
    full = lambda shape: pl.BlockSpec(shape, lambda i: (0,) * len(shape))
    return pl.pallas_call(
        _out_proj_kernel,
        grid=(t // tm,),
        in_specs=[pl.BlockSpec((tm, D_ATTN), lambda i: (i, 0)),
                  pl.BlockSpec((tm, D_SGU), lambda i: (i, 0)),
                  pl.BlockSpec((tm, d), lambda i: (i, 0)),
                  full((1, D_ATTN)), full((D_ATTN + D_SGU, d)), full((1, d))],
        out_specs=[pl.BlockSpec((tm, d), lambda i: (i, 0)),
                   pl.BlockSpec((d, tm), lambda i: (0, i))],
        out_shape=[jax.ShapeDtypeStruct((t, d), F32), jax.ShapeDtypeStruct((d, t), BF16)],
        compiler_params=pltpu.CompilerParams(dimension_semantics=("arbitrary",),
                                             vmem_limit_bytes=VMEM_LIMIT),
        name="out_proj",
    )(attn2, sgu_n, x2, mixw_attn, w_out, ln2)


def _top16(s):
    iota = lax.broadcasted_iota(jnp.int32, s.shape, 0)
    rank = jnp.full(s.shape, PEER_TOPK, jnp.int32)
    vals = []
    cur = s
    for r in range(PEER_TOPK):
        m = jnp.max(cur, axis=0, keepdims=True)
        idx = jnp.min(jnp.where(cur == m, iota, s.shape[0]), axis=0, keepdims=True)
        sel = iota == idx
        rank = jnp.where(sel, r, rank)
        cur = jnp.where(sel, -jnp.inf, cur)
        vals.append(m)
    return rank, vals


def _staircase_counts(v1, v2, tb):
    v2_all = jnp.concatenate(v2, axis=0)
    cands, poss, sizes = [], [], []
    for r1 in range(PEER_TOPK):
        rows = PEER_TOPK if r1 == 0 else 8
        r2 = lax.broadcasted_iota(jnp.int32, (rows, tb), 0)
        cands.append(jnp.where(r2 < PEER_TOPK // (r1 + 1), v1[r1] + v2_all[:rows], -jnp.inf))
        poss.append(r2 + r1 * PEER_TOPK)
        sizes.append(rows)
    cand = jnp.concatenate(cands, axis=0)
    pos = jnp.concatenate(poss, axis=0)
    cur = cand
    picked = jnp.zeros(cand.shape, F32)
    for _ in range(PEER_TOPK):
        m = jnp.max(cur, axis=0, keepdims=True)
        idx = jnp.min(jnp.where(cur == m, pos, PEER_TOPK * PEER_TOPK), axis=0, keepdims=True)
        sel = pos == idx
        picked = jnp.where(sel, 1.0, picked)
        cur = jnp.where(sel, -jnp.inf, cur)
    z = jnp.sum(picked * jnp.exp(cand - (v1[0] + v2[0])), axis=0, keepdims=True)
    counts, lo = [], 0
    for rows in sizes:
        counts.append(jnp.sum(picked[lo:lo + rows], axis=0, keepdims=True))
        lo += rows
    return counts, z


def _route_kernel(hnt_ref, wqt_ref, k1_ref, k2_ref, c1_ref, q1_ref, c2_ref, p2_ref, qt_s):
    tb = hnt_ref.shape[1]
    half = PEER_D_KEY // 2
    qt_s[...] = jnp.dot(wqt_ref[...], hnt_ref[...], preferred_element_type=F32)

    def head(h, carry):
        base = pl.multiple_of(h * PEER_D_KEY, PEER_D_KEY)
        qa = qt_s[pl.ds(base, half), :].astype(BF16)
        qb = qt_s[pl.ds(base + half, half), :].astype(BF16)
        s1 = jnp.dot(k1_ref[h], qa, preferred_element_type=F32)
        s2 = jnp.dot(k2_ref[h], qb, preferred_element_type=F32)
        rank1, v1 = _top16(s1)
        rank2, v2 = _top16(s2)
        counts, z = _staircase_counts(v1, v2, tb)
        c1 = jnp.zeros(s1.shape, F32)
        for r1 in range(PEER_TOPK):
            c1 = jnp.where(rank1 == r1, counts[r1], c1)
        c1_ref[h] = c1
        q1_ref[h] = jnp.exp(s1 - v1[0]) / z
        c2_ref[h] = rank2.astype(F32)
        p2_ref[h] = jnp.exp(s2 - v2[0])
        return carry

    lax.fori_loop(0, PEER_HEADS, head, 0)


def _route(hnt, wqt, keys1, keys2, tb):
    d, t = hnt.shape
    nq = wqt.shape[0]
    full = lambda shape: pl.BlockSpec(shape, lambda i: (0,) * len(shape))
    code = pl.BlockSpec((PEER_HEADS, PEER_KEYS, tb), lambda i: (0, 0, i))
    code_shape = jax.ShapeDtypeStruct((PEER_HEADS, PEER_KEYS, t), F32)
    return pl.pallas_call(
        _route_kernel,
        grid=(t // tb,),
        in_specs=[pl.BlockSpec((d, tb), lambda i: (0, i)), full((nq, d)),
                  full(keys1.shape), full(keys2.shape)],
        out_specs=[code] * 4,
        out_shape=[code_shape] * 4,
        scratch_shapes=[pltpu.VMEM((nq, tb), F32)],
        compiler_params=pltpu.CompilerParams(dimension_semantics=("arbitrary",),
                                             vmem_limit_bytes=VMEM_LIMIT),
        name="route",
    )(hnt, wqt, keys1, keys2)


def _experts_kernel(hnt_ref, u_ref, vt_ref, c1_ref, q1_ref, c2_ref, p2_ref, x1_ref, o_ref,
                    acc_s, z_s):
    j = pl.program_id(1)
    eb = u_ref.shape[0]

    @pl.when(j == 0)
    def _():
        acc_s[...] = jnp.zeros_like(acc_s)

    st = jnp.dot(u_ref[...], hnt_ref[...], preferred_element_type=F32)
    for ii in range(eb // PEER_KEYS):
        i1 = j * (eb // PEER_KEYS) + ii
        rows = slice(ii * PEER_KEYS, (ii + 1) * PEER_KEYS)
        gate = jnp.zeros((PEER_KEYS, st.shape[1]), F32)
        for h in range(PEER_HEADS):
            c1 = c1_ref[h, pl.ds(i1, 1), :]
            q1 = q1_ref[h, pl.ds(i1, 1), :]
            gate = gate + jnp.where(c2_ref[h] < c1, p2_ref[h] * q1, 0.0)
        z_s[rows, :] = (jax.nn.gelu(st[rows]) * gate).astype(BF16)
    acc_s[...] += jnp.dot(vt_ref[...], z_s[...], preferred_element_type=F32)

    @pl.when(j == pl.num_programs(1) - 1)
    def _():
        o_ref[...] = x1_ref[...] + acc_s[...].T


def _experts(hnt, u_tab, vt_tab, codes, x1, tb, eb):
    d, t = hnt.shape
    ne = u_tab.shape[0]
    code = pl.BlockSpec((PEER_HEADS, PEER_KEYS, tb), lambda i, j: (0, 0, i))
    return pl.pallas_call(
        _experts_kernel,
        grid=(t // tb, ne // eb),
        in_specs=[pl.BlockSpec((d, tb), lambda i, j: (0, i)),
                  pl.BlockSpec((eb, d), lambda i, j: (j, 0)),
                  pl.BlockSpec((d, eb), lambda i, j: (0, j)),
                  code, code, code, code,
                  pl.BlockSpec((tb, d), lambda i, j: (i, 0))],
        out_specs=pl.BlockSpec((tb, d), lambda i, j: (i, 0)),
        out_shape=jax.ShapeDtypeStruct((t, d), F32),
        scratch_shapes=[pltpu.VMEM((d, tb), F32), pltpu.VMEM((eb, tb), BF16)],
        compiler_params=pltpu.CompilerParams(dimension_semantics=("arbitrary", "arbitrary"),
                                             vmem_limit_bytes=VMEM_LIMIT),
        name="experts",
    )(hnt, u_tab, vt_tab, *codes, x1)


def kernel(x, ln1_w, w_in, q_norm_w, k_norm_w, sgu_norm_w, sgu_w, sgu_b, mix_norm_w, w_out,
           ln2_w, peer_w_query, peer_keys1, peer_keys2, peer_u, peer_v, rel_bias):
    b, s, d = x.shape
    t = b * s
    bias = _bias_tables(rel_bias)
    for l in range(ln1_w.shape[0]):
        x2 = x.reshape(t, d)
        qkv, sgu_n = _in_proj(x2, ln1_w[l][None], w_in[l].astype(BF16), sgu_norm_w[l], sgu_w[l],
                              sgu_b[l], mix_norm_w[l, D_ATTN:][None], tm=512)
        attn = _attention(qkv.reshape(b, s, 3 * D_ATTN), q_norm_w[l][None], k_norm_w[l][None], bias)
        x1, hnt = _out_proj(attn.reshape(t, D_ATTN), sgu_n, x2, mix_norm_w[l, :D_ATTN][None],
                            w_out[l].astype(BF16), ln2_w[l][None], tm=512)
        codes = _route(hnt, peer_w_query[l].T.astype(BF16), peer_keys1[l].astype(BF16),
                       peer_keys2[l].astype(BF16), tb=256)
        out = _experts(hnt, peer_u[l].astype(BF16), peer_v[l].T.astype(BF16), codes, x1,
                       tb=512, eb=512)
        x = out.reshape(b, s, d)
    return x
```

```python
import functools
import math

import numpy as np
import jax
import jax.numpy as jnp
from jax import lax
from jax.experimental import pallas as pl
from jax.experimental.pallas import tpu as pltpu

F32 = jnp.float32
BF16 = jnp.bfloat16

ATTN_HEADS = 8
HEAD_DIM = 64
D_ATTN = ATTN_HEADS * HEAD_DIM
SGU_GROUPS = 4
SGU_CH = 128
D_SGU = SGU_GROUPS * SGU_CH
CHUNK = 128
DILATED = ((128, 1), (512, 4), (2048, 16))
BLK = 128
NUM_BUCKETS = 32
MAX_DISTANCE = 2048
PEER_HEADS = 8
PEER_KEYS = 128
PEER_D_KEY = 256
PEER_TOPK = 16
EPS = 1e-6
NEG = -1e30

LANES = 128
GATE_TILE = 2 * LANES
GATE_ROWS = 64
VMEM_LIMIT = 56 * 1024 * 1024


def _rms_rows(x, w):
    return x * lax.rsqrt(jnp.mean(x * x, axis=-1, keepdims=True) + EPS) * w


def _bucket_tables():
    qi = np.arange(BLK)[:, None]
    kj = np.arange(2 * BLK)[None, :]
    rel = BLK + qi - kj
    max_exact = NUM_BUCKETS // 2
    buckets, valid = [], []
    for window, dil in DILATED:
        w_sub = window // dil
        dist = np.maximum(rel, 0) * dil
        d_f = np.maximum(dist, max_exact).astype(np.float32)
        large = max_exact + (np.log(d_f / np.float32(max_exact)) / np.float32(math.log(MAX_DISTANCE / max_exact))
                             * np.float32(NUM_BUCKETS - max_exact)).astype(np.int32)
        large = np.minimum(large, NUM_BUCKETS - 1)
        buckets.append(np.where(dist < max_exact, dist, large).astype(np.int32))
        valid.append(((rel >= 0) & (rel <= w_sub)).astype(np.int32))
    return np.stack(buckets), np.stack(valid)


def _bias_kernel(rb_ref, bkt_ref, valid_ref, o_ref):
    h = pl.program_id(1)
    bkt = bkt_ref[...]
    acc = jnp.zeros(bkt.shape, F32)
    for j in range(NUM_BUCKETS):
        acc = jnp.where(bkt == j, rb_ref[j, h], acc)
    o_ref[...] = jnp.where(valid_ref[...] > 0, acc, NEG)


def _bias_tables(rel_bias):
    bkt, valid = _bucket_tables()
    npat = len(DILATED)
    blk = pl.BlockSpec((None, BLK, 2 * BLK), lambda p, h: (p, 0, 0))
    return pl.pallas_call(
        _bias_kernel,
        grid=(npat, ATTN_HEADS),
        in_specs=[pl.BlockSpec(memory_space=pltpu.SMEM), blk, blk],
        out_specs=pl.BlockSpec((None, None, BLK, 2 * BLK), lambda p, h: (p, h, 0, 0)),
        out_shape=jax.ShapeDtypeStruct((npat, ATTN_HEADS, BLK, 2 * BLK), F32),
        name="bias",
    )(rel_bias, jnp.asarray(bkt), jnp.asarray(valid))


def _in_proj_kernel(x_ref, ln1_ref, win_ref, sgnw_ref, sguw_ref, bst_ref, mixw_ref,
                    qkv_ref, sgu_ref, sgu_s):
    tm = x_ref.shape[0]
    h = _rms_rows(x_ref[...], ln1_ref[...]).astype(BF16)
    proj = jnp.dot(h, win_ref[...], preferred_element_type=F32)
    qkv_ref[...] = proj[:, :3 * D_ATTN]
    row = lax.broadcasted_iota(jnp.int32, (CHUNK, CHUNK), 0)
    col = lax.broadcasted_iota(jnp.int32, (CHUNK, CHUNK), 1)
    for g in range(SGU_GROUPS):
        u_lo = 3 * D_ATTN + g * SGU_CH
        v_lo = 3 * D_ATTN + D_SGU + g * SGU_CH
        ua = jax.nn.gelu(proj[:, u_lo:u_lo + SGU_CH])
        va = jax.nn.gelu(proj[:, v_lo:v_lo + SGU_CH])
        vn = _rms_rows(va, sgnw_ref[g:g + 1, :]).astype(BF16)
        w = jnp.where(row >= col, sguw_ref[g], 0.0).astype(BF16)
        bias = bst_ref[:, g:g + 1]
        for c in range(tm // CHUNK):
            rows = slice(c * CHUNK, (c + 1) * CHUNK)
            spatial = jnp.dot(w, vn[rows], preferred_element_type=F32) + bias
            sgu_s[rows, g * SGU_CH:(g + 1) * SGU_CH] = ua[rows] * spatial
    sgu_ref[...] = _rms_rows(sgu_s[...], mixw_ref[...]).astype(BF16)


def _in_proj(x2, ln1, w_in, sgu_norm_w, sgu_w, sgu_b, mixw_sgu, tm):
    t, d = x2.shape
    dproj = w_in.shape[1]
    full = lambda shape: pl.BlockSpec(shape, lambda i: (0,) * len(shape))
    return pl.pallas_call(
        _in_proj_kernel,
        grid=(t // tm,),
        in_specs=[pl.BlockSpec((tm, d), lambda i: (i, 0)),
                  full((1, d)), full((d, dproj)), full((SGU_GROUPS, SGU_CH)),
                  full((SGU_GROUPS, CHUNK, CHUNK)), full((CHUNK, SGU_GROUPS)), full((1, D_SGU))],
        out_specs=[pl.BlockSpec((tm, 3 * D_ATTN), lambda i: (i, 0)),
                   pl.BlockSpec((tm, D_SGU), lambda i: (i, 0))],
        out_shape=[jax.ShapeDtypeStruct((t, 3 * D_ATTN), F32),
                   jax.ShapeDtypeStruct((t, D_SGU), BF16)],
        scratch_shapes=[pltpu.VMEM((tm, D_SGU), F32)],
        compiler_params=pltpu.CompilerParams(dimension_semantics=("arbitrary",),
                                             vmem_limit_bytes=VMEM_LIMIT),
        name="in_proj",
    )(x2, ln1, w_in, sgu_norm_w, sgu_w, sgu_b.T, mixw_sgu)


def _attn_kernel(q_ref, k_ref, v_ref, qw_ref, kw_ref, bias_ref, o_ref,
                 qs, ks, vs, acc_s, m_s, l_s):
    s_len = q_ref.shape[0]
    for hh in range(LANES // HEAD_DIM):
        lanes = slice(hh * HEAD_DIM, (hh + 1) * HEAD_DIM)
        qs[...] = _rms_rows(q_ref[:, lanes], qw_ref[...]) * (HEAD_DIM ** -0.5)
        ks[...] = _rms_rows(k_ref[:, lanes], kw_ref[...])
        vs[...] = v_ref[:, lanes]

        for p, (_, dil) in enumerate(DILATED):
            nblk = s_len // dil // BLK
            shift = dil.bit_length() - 1

            def block(qstart, kstart, nk, bias, p=p, dil=dil):
                qrows = pl.ds(qstart, BLK, stride=dil)
                krows = pl.ds(kstart, nk, stride=dil)
                qb = qs[qrows, :].astype(BF16)
                kb = ks[krows, :].astype(BF16)
                vb = vs[krows, :].astype(BF16)
                s = lax.dot_general(qb, kb, (((1,), (1,)), ((), ())), preferred_element_type=F32)
                logits = jnp.where(bias > 0.1 * NEG, s + bias, NEG)
                m = jnp.max(logits, axis=-1, keepdims=True)
                pe = jnp.exp(logits - m)
                acc_s[p, qrows, :] = jnp.dot(pe.astype(BF16), vb, preferred_element_type=F32)
                m_s[p, qrows, :] = m
                l_s[p, qrows, :] = jnp.sum(pe, axis=-1, keepdims=True)

            def first_block(r, carry, block=block, p=p, hh=hh):
                block(r, r, BLK, bias_ref[p, hh, :, BLK:])
                return carry

            def later_block(i, carry, block=block, p=p, hh=hh, dil=dil, shift=shift):
                r = jnp.bitwise_and(i, dil - 1)
                n = jnp.right_shift(i, shift)
                qstart = r + n * (dil * BLK)
                block(qstart, qstart - dil * BLK, 2 * BLK, bias_ref[p, hh])
                return carry

            lax.fori_loop(0, dil, first_block, 0)
            if nblk > 1:
                lax.fori_loop(dil, dil * nblk, later_block, 0)

        ms = [m_s[p] for p in range(len(DILATED))]
        m_all = functools.reduce(jnp.maximum, ms)
        es = [jnp.exp(m - m_all) for m in ms]
        num = sum(e * acc_s[p] for p, e in enumerate(es))
        den = sum(e * l_s[p] for p, e in enumerate(es))
        o_ref[:, lanes] = num / den


def _attention(qkv3, q_norm_w, k_norm_w, bias):
    b, s, _ = qkv3.shape
    nhp = D_ATTN // LANES
    npat = len(DILATED)
    col = lambda off: pl.BlockSpec((None, s, LANES), lambda bi, hp, off=off: (bi, 0, off + hp))
    return pl.pallas_call(
        _attn_kernel,
        grid=(b, nhp),
        in_specs=[col(0), col(nhp), col(2 * nhp),
                  pl.BlockSpec((1, HEAD_DIM), lambda bi, hp: (0, 0)),
                  pl.BlockSpec((1, HEAD_DIM), lambda bi, hp: (0, 0)),
                  pl.BlockSpec((npat, LANES // HEAD_DIM, BLK, 2 * BLK), lambda bi, hp: (0, hp, 0, 0))],
        out_specs=pl.BlockSpec((None, s, LANES), lambda bi, hp: (bi, 0, hp)),
        out_shape=jax.ShapeDtypeStruct((b, s, D_ATTN), F32),
        scratch_shapes=[pltpu.VMEM((s, HEAD_DIM), F32), pltpu.VMEM((s, HEAD_DIM), F32),
                        pltpu.VMEM((s, HEAD_DIM), F32),
                        pltpu.VMEM((npat, s, HEAD_DIM), F32),
                        pltpu.VMEM((npat, s, 1), F32), pltpu.VMEM((npat, s, 1), F32)],
        compiler_params=pltpu.CompilerParams(dimension_semantics=("arbitrary", "arbitrary"),
                                             vmem_limit_bytes=VMEM_LIMIT),
        name="attn",
    )(qkv3, qkv3, qkv3, q_norm_w, k_norm_w, bias)


def _out_proj_kernel(attn_ref, sgu_ref, x_ref, mixw_ref, wout_ref, ln2_ref, x1_ref, hnt_ref):
    an = _rms_rows(attn_ref[...], mixw_ref[...]).astype(BF16)
    y = jnp.dot(an, wout_ref[:D_ATTN, :], preferred_element_type=F32)
    y = y + jnp.dot(sgu_ref[...], wout_ref[D_ATTN:, :], preferred_element_type=F32)
    x1 = x_ref[...] + y
    x1_ref[...] = x1
    hnt_ref[...] = _rms_rows(x1, ln2_ref[...]).T.astype(BF16)


def _out_proj(attn2, sgu_n, x2, mixw_attn, w_out, ln2, tm):
    t, d = x2.shape
    full = lambda shape: pl.BlockSpec(shape, lambda i: (0,) * len(shape))
    return pl.pallas_call(
        _out_proj_kernel,
        grid=(t // tm,),
        in_specs=[pl.BlockSpec((tm, D_ATTN), lambda i: (i, 0)),
                  pl.BlockSpec((tm, D_SGU), lambda i: (i, 0)),
                  pl.BlockSpec((tm, d), lambda i: (i, 0)),
                  full((1, D_ATTN)), full((D_ATTN + D_SGU, d)), full((1, d))],
        out_specs=[pl.BlockSpec((tm, d), lambda i: (i, 0)),
                   pl.BlockSpec((d, tm), lambda i: (0, i))],
        out_shape=[jax.ShapeDtypeStruct((t, d), F32), jax.ShapeDtypeStruct((d, t), BF16)],
        compiler_params=pltpu.CompilerParams(dimension_semantics=("arbitrary",),
                                             vmem_limit_bytes=VMEM_LIMIT),
        name="out_proj",
    )(attn2, sgu_n, x2, mixw_attn, w_out, ln2)


def _top16(s):
    iota = lax.broadcasted_iota(jnp.int32, s.shape, 0)
    rank = jnp.full(s.shape, PEER_TOPK, jnp.int32)
    vals = []
    cur = s
    for r in range(PEER_TOPK):
        m = jnp.max(cur, axis=0, keepdims=True)
        idx = jnp.min(jnp.where(cur == m, iota, s.shape[0]), axis=0, keepdims=True)
        sel = iota == idx
        rank = jnp.where(sel, r, rank)
        cur = jnp.where(sel, -jnp.inf, cur)
        vals.append(m)
    return rank, vals


def _staircase_counts(v1, v2, tb):
    v2_all = jnp.concatenate(v2, axis=0)
    cands, poss, sizes = [], [], []
    for r1 in range(PEER_TOPK):
        rows = PEER_TOPK if r1 == 0 else 8
        r2 = lax.broadcasted_iota(jnp.int32, (rows, tb), 0)
        cands.append(jnp.where(r2 < PEER_TOPK // (r1 + 1), v1[r1] + v2_all[:rows], -jnp.inf))
        poss.append(r2 + r1 * PEER_TOPK)
        sizes.append(rows)
    cand = jnp.concatenate(cands, axis=0)
    pos = jnp.concatenate(poss, axis=0)
    cur = cand
    picked = jnp.zeros(cand.shape, F32)
    for _ in range(PEER_TOPK):
        m = jnp.max(cur, axis=0, keepdims=True)
        idx = jnp.min(jnp.where(cur == m, pos, PEER_TOPK * PEER_TOPK), axis=0, keepdims=True)
        sel = pos == idx
        picked = jnp.where(sel, 1.0, picked)
        cur = jnp.where(sel, -jnp.inf, cur)
    z = jnp.sum(picked * jnp.exp(cand - (v1[0] + v2[0])), axis=0, keepdims=True)
    counts, lo = [], 0
    for rows in sizes:
        counts.append(jnp.sum(picked[lo:lo + rows], axis=0, keepdims=True))
        lo += rows
    return counts, z


def _route_kernel(hnt_ref, wqt_ref, k1_ref, k2_ref, c1_ref, q1_ref, c2_ref, p2_ref, qt_s):
    tb = hnt_ref.shape[1]
    half = PEER_D_KEY // 2
    qt_s[...] = jnp.dot(wqt_ref[...], hnt_ref[...], preferred_element_type=F32)

    def head(h, carry):
        base = pl.multiple_of(h * PEER_D_KEY, PEER_D_KEY)
        qa = qt_s[pl.ds(base, half), :].astype(BF16)
        qb = qt_s[pl.ds(base + half, half), :].astype(BF16)
        s1 = jnp.dot(k1_ref[h], qa, preferred_element_type=F32)
        s2 = jnp.dot(k2_ref[h], qb, preferred_element_type=F32)
        rank1, v1 = _top16(s1)
        rank2, v2 = _top16(s2)
        counts, z = _staircase_counts(v1, v2, tb)
        c1 = jnp.zeros(s1.shape, F32)
        for r1 in range(PEER_TOPK):
            c1 = jnp.where(rank1 == r1, counts[r1], c1)
        c1_ref[h] = c1
        q1_ref[h] = jnp.exp(s1 - v1[0]) / z
        c2_ref[h] = rank2.astype(F32).astype(BF16)
        p2_ref[h] = jnp.exp(s2 - v2[0]).astype(BF16)
        return carry

    lax.fori_loop(0, PEER_HEADS, head, 0)


def _route(hnt, wqt, keys1, keys2, tb):
    d, t = hnt.shape
    nq = wqt.shape[0]
    full = lambda shape: pl.BlockSpec(shape, lambda i: (0,) * len(shape))
    code = pl.BlockSpec((PEER_HEADS, PEER_KEYS, tb), lambda i: (0, 0, i))
    code_shape = lambda dt: jax.ShapeDtypeStruct((PEER_HEADS, PEER_KEYS, t), dt)
    return pl.pallas_call(
        _route_kernel,
        grid=(t // tb,),
        in_specs=[pl.BlockSpec((d, tb), lambda i: (0, i)), full((nq, d)),
                  full(keys1.shape), full(keys2.shape)],
        out_specs=[code] * 4,
        out_shape=[code_shape(F32), code_shape(F32), code_shape(BF16), code_shape(BF16)],
        scratch_shapes=[pltpu.VMEM((nq, tb), F32)],
        compiler_params=pltpu.CompilerParams(dimension_semantics=("arbitrary",),
                                             vmem_limit_bytes=VMEM_LIMIT),
        name="route",
    )(hnt, wqt, keys1, keys2)


def _experts_kernel(hnt_ref, u_ref, vt_ref, c1_ref, q1_ref, c2_ref, p2_ref, x1_ref, o_ref,
                    acc_s, st_s, z_s, *, n_eb, n_blocks):
    t = pl.program_id(0)
    eb = u_ref.shape[0]
    d_model = vt_ref.shape[0]
    keys_per_block = eb // PEER_KEYS
    j_b =jnp.clip(t - 1, 0, n_blocks - 1) % n_eb
    j_c = jnp.clip(t - 2, 0, n_blocks - 1) % n_eb

    @pl.when(t == 0)
    def _():
        st_s[...] = jnp.zeros_like(st_s)
        z_s[...] = jnp.zeros_like(z_s)
        acc_s[...] = jnp.zeros_like(acc_s)

    @pl.when(jnp.logical_and(t >= 2, j_c == 0))
    def _():
        acc_s[...] = jnp.zeros_like(acc_s)

    def stages(slot):
        rows_of = lambda ref, ii: [ref[h, pl.ds(j_b * keys_per_block + ii, 1), :].astype(BF16)
                                   for h in range(PEER_HEADS)]
        c1_all = [rows_of(c1_ref, ii) for ii in range(keys_per_block)]
        q1_all = [rows_of(q1_ref, ii) for ii in range(keys_per_block)]
        n_cc = st_s.shape[2] // GATE_TILE

        def stage_a(half, cc):
            rows = slice(half * (eb // 2), (half + 1) * (eb // 2))
            cols = slice(cc * GATE_TILE, (cc + 1) * GATE_TILE)
            st_s[slot, rows, cols] = jnp.dot(u_ref[rows, :], hnt_ref[:, cols],
                                             preferred_element_type=F32)

        def stage_b(ii, kh, cc):
            keys = slice(kh * GATE_ROWS, (kh + 1) * GATE_ROWS)
            rows = slice(ii * PEER_KEYS + kh * GATE_ROWS, ii * PEER_KEYS + (kh + 1) * GATE_ROWS)
            cols = slice(cc * GATE_TILE, (cc + 1) * GATE_TILE)
            gate = None
            for h in range(PEER_HEADS):
                term = jnp.where(c2_ref[h, keys, cols] < c1_all[ii][h][:, cols],
                                 p2_ref[h, keys, cols] * q1_all[ii][h][:, cols], jnp.zeros((), BF16))
                gate = term if gate is None else gate + term
            z_s[slot, rows, cols] = jax.nn.gelu(st_s[1 - slot, rows, cols]).astype(BF16) * gate

        def stage_c(quarter, cc):
            rows = slice(quarter * (d_model // 4), (quarter + 1) * (d_model // 4))
            cols = slice(cc * GATE_TILE, (cc + 1) * GATE_TILE)
            acc_s[rows, cols] += jnp.dot(vt_ref[rows, :], z_s[1 - slot, :, cols],
                                         preferred_element_type=F32)

        b_tiles = [(ii, kh, cc) for cc in range(n_cc) for ii in range(keys_per_block)
                   for kh in range(PEER_KEYS // GATE_ROWS)]
        mxu_chunks = ([functools.partial(stage_a, half, cc) for cc in range(n_cc) for half in range(2)]
                      + [functools.partial(stage_c, q, cc) for cc in range(n_cc) for q in range(4)])
        for k, chunk in enumerate(mxu_chunks):
            chunk()
            for tile in b_tiles[k * len(b_tiles) // len(mxu_chunks):
                                (k + 1) * len(b_tiles) // len(mxu_chunks)]:
                stage_b(*tile)

    for parity in range(2):
        pl.when(t % 2 == parity)(functools.partial(stages, parity))

    @pl.when(jnp.logical_and(t >= 2, j_c == n_eb - 1))
    def _():
        o_ref[...] = x1_ref[...] + acc_s[...].T


def _experts(hnt, u_tab, vt_tab, codes, x1, tb, eb):
    d, t = hnt.shape
    ne = u_tab.shape[0]
    n_eb = ne // eb
    n_blocks = (t // tb) * n_eb
    g_a = lambda s: jnp.minimum(s, n_blocks - 1)
    g_b = lambda s: jnp.clip(s - 1, 0, n_blocks - 1)
    g_c = lambda s: jnp.clip(s - 2, 0, n_blocks - 1)
    code = pl.BlockSpec((PEER_HEADS, PEER_KEYS, tb), lambda s: (0, 0, g_b(s) // n_eb))
    assert codes[0].dtype == F32 and codes[1].dtype == F32
    assert codes[2].dtype == BF16 and codes[3].dtype == BF16
    return pl.pallas_call(
        functools.partial(_experts_kernel, n_eb=n_eb, n_blocks=n_blocks),
        grid=(n_blocks + 2,),
        in_specs=[pl.BlockSpec((d, tb), lambda s: (0, g_a(s) // n_eb)),
                  pl.BlockSpec((eb, d), lambda s: (g_a(s) % n_eb, 0)),
                  pl.BlockSpec((d, eb), lambda s: (0, g_c(s) % n_eb)),
                  code, code, code, code,
                  pl.BlockSpec((tb, d), lambda s: (g_c(s) // n_eb, 0))],
        out_specs=pl.BlockSpec((tb, d), lambda s: (g_c(s) // n_eb, 0)),
        out_shape=jax.ShapeDtypeStruct((t, d), F32),
        scratch_shapes=[pltpu.VMEM((d, tb), F32), pltpu.VMEM((2, eb, tb), F32),
                        pltpu.VMEM((2, eb, tb), BF16)],
        compiler_params=pltpu.CompilerParams(dimension_semantics=("arbitrary",),
                                             vmem_limit_bytes=VMEM_LIMIT),
        name="experts",
    )(hnt, u_tab, vt_tab, *codes, x1)


def kernel(x, ln1_w, w_in, q_norm_w, k_norm_w, sgu_norm_w, sgu_w, sgu_b, mix_norm_w, w_out,
           ln2_w, peer_w_query, peer_keys1, peer_keys2, peer_u, peer_v, rel_bias):
    b, s, d = x.shape
    t = b * s
    bias = _bias_tables(rel_bias)
    for l in range(ln1_w.shape[0]):
        x2 = x.reshape(t, d)
        qkv, sgu_n = _in_proj(x2, ln1_w[l][None], w_in[l].astype(BF16), sgu_norm_w[l], sgu_w[l],
                              sgu_b[l], mix_norm_w[l, D_ATTN:][None], tm=512)
        attn = _attention(qkv.reshape(b, s, 3 * D_ATTN), q_norm_w[l][None], k_norm_w[l][None], bias)
        x1, hnt = _out_proj(attn.reshape(t, D_ATTN), sgu_n, x2, mix_norm_w[l, :D_ATTN][None],
                            w_out[l].astype(BF16), ln2_w[l][None], tm=512)
        codes = _route(hnt, peer_w_query[l].T.astype(BF16), peer_keys1[l].astype(BF16),
                       peer_keys2[l].astype(BF16), tb=256)
        out = _experts(hnt, peer_u[l].astype(BF16), peer_v[l].T.astype(BF16), codes, x1,
                       tb=512, eb=512)
        x = out.reshape(b, s, d)
    return x
```

```python
import functools
import math

import numpy as np
import jax
import jax.numpy as jnp
from jax import lax
from jax.experimental import pallas as pl
from jax.experimental.pallas import tpu as pltpu

F32 = jnp.float32
BF16 = jnp.bfloat16

ATTN_HEADS = 8
HEAD_DIM = 64
D_ATTN = ATTN_HEADS * HEAD_DIM
SGU_GROUPS = 4
SGU_CH = 128
D_SGU = SGU_GROUPS * SGU_CH
CHUNK = 128
DILATED = ((128, 1), (512, 4), (2048, 16))
BLK = 128
NUM_BUCKETS = 32
MAX_DISTANCE = 2048
PEER_HEADS = 8
PEER_KEYS = 128
PEER_D_KEY = 256
PEER_TOPK = 16
EPS = 1e-6
NEG = -1e30

LANES = 128
BF16_ROWS = 16
MXU_COLS = 256
CODE_ROW_BLOCK = 8
EXPERT_BLOCK = 512
VMEM_LIMIT = 56 * 1024 * 1024


def _rms_rows(x, w):
    return x * lax.rsqrt(jnp.mean(x * x, axis=-1, keepdims=True) + EPS) * w


def _bucket_tables():
    qi = np.arange(BLK)[:, None]
    kj = np.arange(2 * BLK)[None, :]
    rel = BLK + qi - kj
    max_exact = NUM_BUCKETS // 2
    buckets, valid = [], []
    for window, dil in DILATED:
        w_sub = window // dil
        dist = np.maximum(rel, 0) * dil
        d_f = np.maximum(dist, max_exact).astype(np.float32)
        large = max_exact + (np.log(d_f / np.float32(max_exact)) / np.float32(math.log(MAX_DISTANCE / max_exact))
                             * np.float32(NUM_BUCKETS - max_exact)).astype(np.int32)
        large = np.minimum(large, NUM_BUCKETS - 1)
        buckets.append(np.where(dist < max_exact, dist, large).astype(np.int32))
        valid.append(((rel >= 0) & (rel <= w_sub)).astype(np.int32))
    return np.stack(buckets), np.stack(valid)


def _bias_kernel(rb_ref, bkt_ref, valid_ref, o_ref):
    h = pl.program_id(1)
    bkt = bkt_ref[...]
    acc = jnp.zeros(bkt.shape, F32)
    for j in range(NUM_BUCKETS):
        acc = jnp.where(bkt == j, rb_ref[j, h], acc)
    o_ref[...] = jnp.where(valid_ref[...] > 0, acc, NEG)


def _bias_tables(rel_bias):
    bkt, valid = _bucket_tables()
    npat = len(DILATED)
    blk = pl.BlockSpec((None, BLK, 2 * BLK), lambda p, h: (p, 0, 0))
    return pl.pallas_call(
        _bias_kernel,
        grid=(npat, ATTN_HEADS),
        in_specs=[pl.BlockSpec(memory_space=pltpu.SMEM), blk, blk],
        out_specs=pl.BlockSpec((None, None, BLK, 2 * BLK), lambda p, h: (p, h, 0, 0)),
        out_shape=jax.ShapeDtypeStruct((npat, ATTN_HEADS, BLK, 2 * BLK), F32),
        name="bias",
    )(rel_bias, jnp.asarray(bkt), jnp.asarray(valid))


def _in_proj_kernel(x_ref, ln1_ref, win_ref, sgnw_ref, sguw_ref, bst_ref, mixw_ref,
                    qkv_ref, sgu_ref, sgu_s):
    tm = x_ref.shape[0]
    h = _rms_rows(x_ref[...], ln1_ref[...]).astype(BF16)
    proj = jnp.dot(h, win_ref[...], preferred_element_type=F32)
    qkv_ref[...] = proj[:, :3 * D_ATTN]
    row = lax.broadcasted_iota(jnp.int32, (CHUNK, CHUNK), 0)
    col = lax.broadcasted_iota(jnp.int32, (CHUNK, CHUNK), 1)
    for g in range(SGU_GROUPS):
        u_lo = 3 * D_ATTN + g * SGU_CH
        v_lo = 3 * D_ATTN + D_SGU + g * SGU_CH
        ua = jax.nn.gelu(proj[:, u_lo:u_lo + SGU_CH])
        va = jax.nn.gelu(proj[:, v_lo:v_lo + SGU_CH])
        vn = _rms_rows(va, sgnw_ref[g:g + 1, :]).astype(BF16)
        w = jnp.where(row >= col, sguw_ref[g], 0.0).astype(BF16)
        bias = bst_ref[:, g:g + 1]
        for c in range(tm // CHUNK):
            rows = slice(c * CHUNK, (c + 1) * CHUNK)
            spatial = jnp.dot(w, vn[rows], preferred_element_type=F32) + bias
            sgu_s[rows, g * SGU_CH:(g + 1) * SGU_CH] = ua[rows] * spatial
    sgu_ref[...] = _rms_rows(sgu_s[...], mixw_ref[...]).astype(BF16)


def _in_proj(x2, ln1, w_in, sgu_norm_w, sgu_w, sgu_b, mixw_sgu, tm):
    t, d = x2.shape
    dproj = w_in.shape[1]
    full = lambda shape: pl.BlockSpec(shape, lambda i: (0,) * len(shape))
    return pl.pallas_call(
        _in_proj_kernel,
        grid=(t // tm,),
        in_specs=[pl.BlockSpec((tm, d), lambda i: (i, 0)),
                  full((1, d)), full((d, dproj)), full((SGU_GROUPS, SGU_CH)),
                  full((SGU_GROUPS, CHUNK, CHUNK)), full((CHUNK, SGU_GROUPS)), full((1, D_SGU))],
        out_specs=[pl.BlockSpec((tm, 3 * D_ATTN), lambda i: (i, 0)),
                   pl.BlockSpec((tm, D_SGU), lambda i: (i, 0))],
        out_shape=[jax.ShapeDtypeStruct((t, 3 * D_ATTN), F32),
                   jax.ShapeDtypeStruct((t, D_SGU), BF16)],
        scratch_shapes=[pltpu.VMEM((tm, D_SGU), F32)],
        compiler_params=pltpu.CompilerParams(dimension_semantics=("arbitrary",),
                                             vmem_limit_bytes=VMEM_LIMIT),
        name="in_proj",
    )(x2, ln1, w_in, sgu_norm_w, sgu_w, sgu_b.T, mixw_sgu)


def _attn_kernel(q_ref, k_ref, v_ref, qw_ref, kw_ref, bias_ref, o_ref,
                 qs, ks, vs, acc_s, m_s, l_s):
    s_len = q_ref.shape[0]
    for hh in range(LANES // HEAD_DIM):
        lanes = slice(hh * HEAD_DIM, (hh + 1) * HEAD_DIM)
        qs[...] = _rms_rows(q_ref[:, lanes], qw_ref[...]) * (HEAD_DIM ** -0.5)
        ks[...] = _rms_rows(k_ref[:, lanes], kw_ref[...])
        vs[...] = v_ref[:, lanes]

        for p, (_, dil) in enumerate(DILATED):
            nblk = s_len // dil // BLK
            shift = dil.bit_length() - 1

            def block(qstart, kstart, nk, bias, p=p, dil=dil):
                qrows = pl.ds(qstart, BLK, stride=dil)
                krows = pl.ds(kstart, nk, stride=dil)
                qb = qs[qrows, :].astype(BF16)
                kb = ks[krows, :].astype(BF16)
                vb = vs[krows, :].astype(BF16)
                s = lax.dot_general(qb, kb, (((1,), (1,)), ((), ())), preferred_element_type=F32)
                logits = jnp.where(bias > 0.1 * NEG, s + bias, NEG)
                m = jnp.max(logits, axis=-1, keepdims=True)
                pe = jnp.exp(logits - m)
                acc_s[p, qrows, :] = jnp.dot(pe.astype(BF16), vb, preferred_element_type=F32)
                m_s[p, qrows, :] = m
                l_s[p, qrows, :] = jnp.sum(pe, axis=-1, keepdims=True)

            def first_block(r, carry, block=block, p=p, hh=hh):
                block(r, r, BLK, bias_ref[p, hh, :, BLK:])
                return carry

            def later_block(i, carry, block=block, p=p, hh=hh, dil=dil, shift=shift):
                r = jnp.bitwise_and(i, dil - 1)
                n = jnp.right_shift(i, shift)
                qstart = r + n * (dil * BLK)
                block(qstart, qstart - dil * BLK, 2 * BLK, bias_ref[p, hh])
                return carry

            lax.fori_loop(0, dil, first_block, 0)
            if nblk > 1:
                lax.fori_loop(dil, dil * nblk, later_block, 0)

        ms = [m_s[p] for p in range(len(DILATED))]
        m_all = functools.reduce(jnp.maximum, ms)
        es = [jnp.exp(m - m_all) for m in ms]
        num = sum(e * acc_s[p] for p, e in enumerate(es))
        den = sum(e * l_s[p] for p, e in enumerate(es))
        o_ref[:, lanes] = num / den


def _attention(qkv3, q_norm_w, k_norm_w, bias):
    b, s, _ = qkv3.shape
    nhp = D_ATTN // LANES
    npat = len(DILATED)
    col = lambda off: pl.BlockSpec((None, s, LANES), lambda bi, hp, off=off: (bi, 0, off + hp))
    return pl.pallas_call(
        _attn_kernel,
        grid=(b, nhp),
        in_specs=[col(0), col(nhp), col(2 * nhp),
                  pl.BlockSpec((1, HEAD_DIM), lambda bi, hp: (0, 0)),
                  pl.BlockSpec((1, HEAD_DIM), lambda bi, hp: (0, 0)),
                  pl.BlockSpec((npat, LANES // HEAD_DIM, BLK, 2 * BLK), lambda bi, hp: (0, hp, 0, 0))],
        out_specs=pl.BlockSpec((None, s, LANES), lambda bi, hp: (bi, 0, hp)),
        out_shape=jax.ShapeDtypeStruct((b, s, D_ATTN), F32),
        scratch_shapes=[pltpu.VMEM((s, HEAD_DIM), F32), pltpu.VMEM((s, HEAD_DIM), F32),
                        pltpu.VMEM((s, HEAD_DIM), F32),
                        pltpu.VMEM((npat, s, HEAD_DIM), F32),
                        pltpu.VMEM((npat, s, 1), F32), pltpu.VMEM((npat, s, 1), F32)],
        compiler_params=pltpu.CompilerParams(dimension_semantics=("arbitrary", "arbitrary"),
                                             vmem_limit_bytes=VMEM_LIMIT),
        name="attn",
    )(qkv3, qkv3, qkv3, q_norm_w, k_norm_w, bias)


def _out_proj_kernel(attn_ref, sgu_ref, x_ref, mixw_ref, wout_ref, ln2_ref, x1_ref, hnt_ref):
    an = _rms_rows(attn_ref[...], mixw_ref[...]).astype(BF16)
    y = jnp.dot(an, wout_ref[:D_ATTN, :], preferred_element_type=F32)
    y = y + jnp.dot(sgu_ref[...], wout_ref[D_ATTN:, :], preferred_element_type=F32)
    x1 = x_ref[...] + y
    x1_ref[...] = x1
    hnt_ref[...] = _rms_rows(x1, ln2_ref[...]).T.astype(BF16)


def _out_proj(attn2, sgu_n, x2, mixw_attn, w_out, ln2, tm):
    t, d = x2.shape
    full = lambda shape: pl.BlockSpec(shape, lambda i: (0,) * len(shape))
    return pl.pallas_call(
        _out_proj_kernel,
        grid=(t // tm,),
        in_specs=[pl.BlockSpec((tm, D_ATTN), lambda i: (i, 0)),
                  pl.BlockSpec((tm, D_SGU), lambda i: (i, 0)),
                  pl.BlockSpec((tm, d), lambda i: (i, 0)),
                  full((1, D_ATTN)), full((D_ATTN + D_SGU, d)), full((1, d))],
        out_specs=[pl.BlockSpec((tm, d), lambda i: (i, 0)),
                   pl.BlockSpec((d, tm), lambda i: (0, i))],
        out_shape=[jax.ShapeDtypeStruct((t, d), F32), jax.ShapeDtypeStruct((d, t), BF16)],
        compiler_params=pltpu.CompilerParams(dimension_semantics=("arbitrary",),
                                             vmem_limit_bytes=VMEM_LIMIT),
        name="out_proj",
    )(attn2, sgu_n, x2, mixw_attn, w_out, ln2)


def _top16(s):
    iota = lax.broadcasted_iota(jnp.int32, s.shape, 0)
    rank = jnp.full(s.shape, PEER_TOPK, jnp.int32)
    vals = []
    cur = s
    for r in range(PEER_TOPK):
        m = jnp.max(cur, axis=0, keepdims=True)
        idx = jnp.min(jnp.where(cur == m, iota, s.shape[0]), axis=0, keepdims=True)
        sel = iota == idx
        rank = jnp.where(sel, r, rank)
        cur = jnp.where(sel, -jnp.inf, cur)
        vals.append(m)
    return rank, vals


def _staircase_counts(v1, v2, tb):
    v2_all = jnp.concatenate(v2, axis=0)
    cands, poss, sizes = [], [], []
    for r1 in range(PEER_TOPK):
        rows = PEER_TOPK if r1 == 0 else 8
        r2 = lax.broadcasted_iota(jnp.int32, (rows, tb), 0)
        cands.append(jnp.where(r2 < PEER_TOPK // (r1 + 1), v1[r1] + v2_all[:rows], -jnp.inf))
        poss.append(r2 + r1 * PEER_TOPK)
        sizes.append(rows)
    cand = jnp.concatenate(cands, axis=0)
    pos = jnp.concatenate(poss, axis=0)
    cur = cand
    picked = jnp.zeros(cand.shape, F32)
    for _ in range(PEER_TOPK):
        m = jnp.max(cur, axis=0, keepdims=True)
        idx = jnp.min(jnp.where(cur == m, pos, PEER_TOPK * PEER_TOPK), axis=0, keepdims=True)
        sel = pos == idx
        picked = jnp.where(sel, 1.0, picked)
        cur = jnp.where(sel, -jnp.inf, cur)
    z = jnp.sum(picked * jnp.exp(cand - (v1[0] + v2[0])), axis=0, keepdims=True)
    counts, lo = [], 0
    for rows in sizes:
        counts.append(jnp.sum(picked[lo:lo + rows], axis=0, keepdims=True))
        lo += rows
    return counts, z


def _route_kernel(hnt_ref, wqt_ref, k1_ref, k2_ref, c1_ref, q1_ref, c2_ref, p2_ref, qt_s):
    tb = hnt_ref.shape[1]
    half = PEER_D_KEY // 2
    qt_s[...] = jnp.dot(wqt_ref[...], hnt_ref[...], preferred_element_type=F32)

    def head(h, carry):
        base = pl.multiple_of(h * PEER_D_KEY, PEER_D_KEY)
        qa = qt_s[pl.ds(base, half), :].astype(BF16)
        qb = qt_s[pl.ds(base + half, half), :].astype(BF16)
        s1 = jnp.dot(k1_ref[h], qa, preferred_element_type=F32)
        s2 = jnp.dot(k2_ref[h], qb, preferred_element_type=F32)
        rank1, v1 = _top16(s1)
        rank2, v2 = _top16(s2)
        counts, z = _staircase_counts(v1, v2, tb)
        c1 = jnp.zeros(s1.shape, F32)
        for r1 in range(PEER_TOPK):
            c1 = jnp.where(rank1 == r1, counts[r1], c1)
        c1_ref[h] = c1
        q1_ref[h] = jnp.exp(s1 - v1[0]) / z
        c2_ref[h] = rank2.astype(F32).astype(BF16)
        p2_ref[h] = jnp.exp(s2 - v2[0]).astype(BF16)
        return carry

    lax.fori_loop(0, PEER_HEADS, head, 0)


def _route(hnt, wqt, keys1, keys2, tb):
    d, t = hnt.shape
    nq = wqt.shape[0]
    full = lambda shape: pl.BlockSpec(shape, lambda i: (0,) * len(shape))
    code = pl.BlockSpec((PEER_HEADS, PEER_KEYS, tb), lambda i: (0, 0, i))
    code_shape = lambda dt: jax.ShapeDtypeStruct((PEER_HEADS, PEER_KEYS, t), dt)
    return pl.pallas_call(
        _route_kernel,
        grid=(t // tb,),
        in_specs=[pl.BlockSpec((d, tb), lambda i: (0, i)), full((nq, d)),
                  full(keys1.shape), full(keys2.shape)],
        out_specs=[code] * 4,
        out_shape=[code_shape(F32), code_shape(F32), code_shape(BF16), code_shape(BF16)],
        scratch_shapes=[pltpu.VMEM((nq, tb), F32)],
        compiler_params=pltpu.CompilerParams(dimension_semantics=("arbitrary",),
                                             vmem_limit_bytes=VMEM_LIMIT),
        name="route",
    )(hnt, wqt, keys1, keys2)


def _experts_kernel(hnt_ref, u_ref, vt_ref, c1_ref, q1_ref, c2_ref, p2_ref, x1_ref, o_ref,
                    acc_s, st_s, z_s, c2_s, p2_s, *, n_eb, n_blocks):
    t = pl.program_id(0)
    eb = u_ref.shape[0]
    tb = hnt_ref.shape[1]
    d_model = vt_ref.shape[0]
    keys_per_block = eb // PEER_KEYS
    j_b = jnp.clip(t - 1, 0, n_blocks - 1) % n_eb
    j_c = jnp.clip(t - 2, 0, n_blocks - 1) % n_eb

    @pl.when(t == 0)
    def _():
        st_s[...] = jnp.zeros_like(st_s)
        z_s[...] = jnp.zeros_like(z_s)
        acc_s[...] = jnp.zeros_like(acc_s)

    @pl.when(jnp.logical_and(t >= 2, j_c == 0))
    def _():
        acc_s[...] = jnp.zeros_like(acc_s)

    @pl.when(j_b == 0)
    def _():
        c2_s[...] = c2_ref[...]
        p2_s[...] = p2_ref[...]

    def stages(slot):
        row0 = (j_b * keys_per_block) % CODE_ROW_BLOCK
        rows_of = lambda ref, ii: [ref[h, pl.ds(row0 + ii, 1), :] for h in range(PEER_HEADS)]
        c1_all = [rows_of(c1_ref, ii) for ii in range(keys_per_block)]
        q1_all = [rows_of(q1_ref, ii) for ii in range(keys_per_block)]
        packed = (PEER_KEYS // BF16_ROWS, BF16_ROWS, LANES)
        row_bf16 = lambda r: jnp.broadcast_to(r, (BF16_ROWS, LANES)).astype(BF16)[None]

        def stage_a(half, cc):
            rows = slice(half * (eb // 2), (half + 1) * (eb // 2))
            cols = slice(cc * MXU_COLS, (cc + 1) * MXU_COLS)
            st_s[slot, rows, cols] = jnp.dot(u_ref[rows, :], hnt_ref[:, cols],
                                             preferred_element_type=F32)

        def stage_b(ii, cc):
            rows = slice(ii * PEER_KEYS, (ii + 1) * PEER_KEYS)
            cols = slice(cc * LANES, (cc + 1) * LANES)
            gate = None
            for h in range(PEER_HEADS):
                c1 = row_bf16(c1_all[ii][h][:, cols])
                q1 = row_bf16(q1_all[ii][h][:, cols])
                term = jnp.where(c2_s[h, :, cols].reshape(packed) < c1,
                                 p2_s[h, :, cols].reshape(packed) * q1, jnp.zeros((), BF16))
                gate = term if gate is None else gate + term
            act = jax.nn.gelu(st_s[1 - slot, rows, cols]).astype(BF16).reshape(packed)
            z_s[slot, rows, cols] = (act * gate).reshape(PEER_KEYS, LANES)

        def stage_c(quarter, cc):
            rows = slice(quarter * (d_model // 4), (quarter + 1) * (d_model // 4))
            cols = slice(cc * MXU_COLS, (cc + 1) * MXU_COLS)
            acc_s[rows, cols] += jnp.dot(vt_ref[rows, :], z_s[1 - slot, :, cols],
                                         preferred_element_type=F32)

        n_mc = tb // MXU_COLS
        b_tiles = [(ii, cc) for cc in range(tb // LANES) for ii in range(keys_per_block)]
        mxu_chunks = ([functools.partial(stage_a, half, cc) for cc in range(n_mc) for half in range(2)]
                      + [functools.partial(stage_c, q, cc) for cc in range(n_mc) for q in range(4)])
        for k, chunk in enumerate(mxu_chunks):
            chunk()
            for tile in b_tiles[k * len(b_tiles) // len(mxu_chunks):
                                (k + 1) * len(b_tiles) // len(mxu_chunks)]:
                stage_b(*tile)

    for parity in range(2):
        pl.when(t % 2 == parity)(functools.partial(stages, parity))

    @pl.when(jnp.logical_and(t >= 2, j_c == n_eb - 1))
    def _():
        o_ref[...] = x1_ref[...] + acc_s[...].T


def _experts(hnt, u_tab, vt_blocks, codes, x1, tb):
    d, t = hnt.shape
    n_eb, _, eb = vt_blocks.shape
    keys_per_block = eb // PEER_KEYS
    assert CODE_ROW_BLOCK % keys_per_block == 0
    n_blocks = (t // tb) * n_eb
    g_a = lambda s: jnp.minimum(s, n_blocks - 1)
    g_b = lambda s: jnp.clip(s - 1, 0, n_blocks - 1)
    g_c = lambda s: jnp.clip(s - 2, 0, n_blocks - 1)
    code_rows = pl.BlockSpec(
        (PEER_HEADS, CODE_ROW_BLOCK, tb),
        lambda s: (0, (g_b(s) % n_eb) * keys_per_block // CODE_ROW_BLOCK, g_b(s) // n_eb))
    code_keys = pl.BlockSpec((PEER_HEADS, PEER_KEYS, tb), lambda s: (0, 0, g_b(s) // n_eb))
    assert codes[0].dtype == F32 and codes[1].dtype == F32
    assert codes[2].dtype == BF16 and codes[3].dtype == BF16
    return pl.pallas_call(
        functools.partial(_experts_kernel, n_eb=n_eb, n_blocks=n_blocks),
        grid=(n_blocks + 2,),
        in_specs=[pl.BlockSpec((d, tb), lambda s: (0, g_a(s) // n_eb)),
                  pl.BlockSpec((eb, d), lambda s: (g_a(s) % n_eb, 0)),
                  pl.BlockSpec((None, d, eb), lambda s: (g_c(s) % n_eb, 0, 0)),
                  code_rows, code_rows, code_keys, code_keys,
                  pl.BlockSpec((tb, d), lambda s: (g_c(s) // n_eb, 0))],
        out_specs=pl.BlockSpec((tb, d), lambda s: (g_c(s) // n_eb, 0)),
        out_shape=jax.ShapeDtypeStruct((t, d), F32),
        scratch_shapes=[pltpu.VMEM((d, tb), F32), pltpu.VMEM((2, eb, tb), F32),
                        pltpu.VMEM((2, eb, tb), BF16),
                        pltpu.VMEM((PEER_HEADS, PEER_KEYS, tb), BF16),
                        pltpu.VMEM((PEER_HEADS, PEER_KEYS, tb), BF16)],
        compiler_params=pltpu.CompilerParams(dimension_semantics=("arbitrary",),
                                             vmem_limit_bytes=VMEM_LIMIT),
        name="experts",
    )(hnt, u_tab, vt_blocks, *codes, x1)


def kernel(x, ln1_w, w_in, q_norm_w, k_norm_w, sgu_norm_w, sgu_w, sgu_b, mix_norm_w, w_out,
           ln2_w, peer_w_query, peer_keys1, peer_keys2, peer_u, peer_v, rel_bias):
    b, s, d = x.shape
    t = b * s
    bias = _bias_tables(rel_bias)
    for l in range(ln1_w.shape[0]):
        x2 = x.reshape(t, d)
        qkv, sgu_n = _in_proj(x2, ln1_w[l][None], w_in[l].astype(BF16), sgu_norm_w[l], sgu_w[l],
                              sgu_b[l], mix_norm_w[l, D_ATTN:][None], tm=512)
        attn = _attention(qkv.reshape(b, s, 3 * D_ATTN), q_norm_w[l][None], k_norm_w[l][None], bias)
        x1, hnt = _out_proj(attn.reshape(t, D_ATTN), sgu_n, x2, mix_norm_w[l, :D_ATTN][None],
                            w_out[l].astype(BF16), ln2_w[l][None], tm=512)
        codes = _route(hnt, peer_w_query[l].T.astype(BF16), peer_keys1[l].astype(BF16),
                       peer_keys2[l].astype(BF16), tb=256)
        vt_blocks = peer_v[l].reshape(-1, EXPERT_BLOCK, d).transpose(0, 2, 1).astype(BF16)
        out = _experts(hnt, peer_u[l].astype(BF16), vt_blocks, codes, x1, tb=1024)
        x = out.reshape(b, s, d)
    return x
```

```python
import functools
import math

import numpy as np
import jax
import jax.numpy as jnp
from jax import lax
from jax.experimental import pallas as pl
from jax.experimental.pallas import tpu as pltpu

F32 = jnp.float32
BF16 = jnp.bfloat16

ATTN_HEADS = 8
HEAD_DIM = 64
D_ATTN = ATTN_HEADS * HEAD_DIM
SGU_GROUPS = 4
SGU_CH = 128
D_SGU = SGU_GROUPS * SGU_CH
CHUNK = 128
DILATED = ((128, 1), (512, 4), (2048, 16))
BLK = 128
NUM_BUCKETS = 32
MAX_DISTANCE = 2048
PEER_HEADS = 8
PEER_KEYS = 128
PEER_D_KEY = 256
PEER_TOPK = 16
EPS = 1e-6
NEG = -1e30

LANES = 128
BF16_ROWS = 16
MXU_COLS = 256
CODE_ROW_BLOCK = 8
EXPERT_BLOCK = 1024
EXPERT_TOKENS = 512
VMEM_LIMIT = 56 * 1024 * 1024


def _rms_rows(x, w):
    return x * lax.rsqrt(jnp.mean(x * x, axis=-1, keepdims=True) + EPS) * w


def _bucket_tables():
    qi = np.arange(BLK)[:, None]
    kj = np.arange(2 * BLK)[None, :]
    rel = BLK + qi - kj
    max_exact = NUM_BUCKETS // 2
    buckets, valid = [], []
    for window, dil in DILATED:
        w_sub = window // dil
        dist = np.maximum(rel, 0) * dil
        d_f = np.maximum(dist, max_exact).astype(np.float32)
        large = max_exact + (np.log(d_f / np.float32(max_exact)) / np.float32(math.log(MAX_DISTANCE / max_exact))
                             * np.float32(NUM_BUCKETS - max_exact)).astype(np.int32)
        large = np.minimum(large, NUM_BUCKETS - 1)
        buckets.append(np.where(dist < max_exact, dist, large).astype(np.int32))
        valid.append(((rel >= 0) & (rel <= w_sub)).astype(np.int32))
    return np.stack(buckets), np.stack(valid)


def _bias_kernel(rb_ref, bkt_ref, valid_ref, o_ref):
    h = pl.program_id(1)
    bkt = bkt_ref[...]
    acc = jnp.zeros(bkt.shape, F32)
    for j in range(NUM_BUCKETS):
        acc = jnp.where(bkt == j, rb_ref[j, h], acc)
    o_ref[...] = jnp.where(valid_ref[...] > 0, acc, NEG)


def _bias_tables(rel_bias):
    bkt, valid = _bucket_tables()
    npat = len(DILATED)
    blk = pl.BlockSpec((None, BLK, 2 * BLK), lambda p, h: (p, 0, 0))
    return pl.pallas_call(
        _bias_kernel,
        grid=(npat, ATTN_HEADS),
        in_specs=[pl.BlockSpec(memory_space=pltpu.SMEM), blk, blk],
        out_specs=pl.BlockSpec((None, None, BLK, 2 * BLK), lambda p, h: (p, h, 0, 0)),
        out_shape=jax.ShapeDtypeStruct((npat, ATTN_HEADS, BLK, 2 * BLK), F32),
        name="bias",
    )(rel_bias, jnp.asarray(bkt), jnp.asarray(valid))


def _in_proj_kernel(x_ref, ln1_ref, win_ref, sgnw_ref, sguw_ref, bst_ref, mixw_ref,
                    qkv_ref, sgu_ref, sgu_s):
    tm = x_ref.shape[0]
    h = _rms_rows(x_ref[...], ln1_ref[...]).astype(BF16)
    proj = jnp.dot(h, win_ref[...], preferred_element_type=F32)
    qkv_ref[...] = proj[:, :3 * D_ATTN]
    row = lax.broadcasted_iota(jnp.int32, (CHUNK, CHUNK), 0)
    col = lax.broadcasted_iota(jnp.int32, (CHUNK, CHUNK), 1)
    for g in range(SGU_GROUPS):
        u_lo = 3 * D_ATTN + g * SGU_CH
        v_lo = 3 * D_ATTN + D_SGU + g * SGU_CH
        ua = jax.nn.gelu(proj[:, u_lo:u_lo + SGU_CH])
        va = jax.nn.gelu(proj[:, v_lo:v_lo + SGU_CH])
        vn = _rms_rows(va, sgnw_ref[g:g + 1, :]).astype(BF16)
        w = jnp.where(row >= col, sguw_ref[g], 0.0).astype(BF16)
        bias = bst_ref[:, g:g + 1]
        for c in range(tm // CHUNK):
            rows = slice(c * CHUNK, (c + 1) * CHUNK)
            spatial = jnp.dot(w, vn[rows], preferred_element_type=F32) + bias
            sgu_s[rows, g * SGU_CH:(g + 1) * SGU_CH] = ua[rows] * spatial
    sgu_ref[...] = _rms_rows(sgu_s[...], mixw_ref[...]).astype(BF16)


def _in_proj(x2, ln1, w_in, sgu_norm_w, sgu_w, sgu_b, mixw_sgu, tm):
    t, d = x2.shape
    dproj = w_in.shape[1]
    full = lambda shape: pl.BlockSpec(shape, lambda i: (0,) * len(shape))
    return pl.pallas_call(
        _in_proj_kernel,
        grid=(t // tm,),
        in_specs=[pl.BlockSpec((tm, d), lambda i: (i, 0)),
                  full((1, d)), full((d, dproj)), full((SGU_GROUPS, SGU_CH)),
                  full((SGU_GROUPS, CHUNK, CHUNK)), full((CHUNK, SGU_GROUPS)), full((1, D_SGU))],
        out_specs=[pl.BlockSpec((tm, 3 * D_ATTN), lambda i: (i, 0)),
                   pl.BlockSpec((tm, D_SGU), lambda i: (i, 0))],
        out_shape=[jax.ShapeDtypeStruct((t, 3 * D_ATTN), F32),
                   jax.ShapeDtypeStruct((t, D_SGU), BF16)],
        scratch_shapes=[pltpu.VMEM((tm, D_SGU), F32)],
        compiler_params=pltpu.CompilerParams(dimension_semantics=("arbitrary",),
                                             vmem_limit_bytes=VMEM_LIMIT),
        name="in_proj",
    )(x2, ln1, w_in, sgu_norm_w, sgu_w, sgu_b.T, mixw_sgu)


def _attn_kernel(q_ref, k_ref, v_ref, qw_ref, kw_ref, bias_ref, o_ref,
                 qs, ks, vs, acc_s, m_s, l_s):
    s_len = q_ref.shape[0]
    for hh in range(LANES // HEAD_DIM):
        lanes = slice(hh * HEAD_DIM, (hh + 1) * HEAD_DIM)
        qs[...] = _rms_rows(q_ref[:, lanes], qw_ref[...]) * (HEAD_DIM ** -0.5)
        ks[...] = _rms_rows(k_ref[:, lanes], kw_ref[...])
        vs[...] = v_ref[:, lanes]

        for p, (_, dil) in enumerate(DILATED):
            nblk = s_len // dil // BLK
            shift = dil.bit_length() - 1

            def block(qstart, kstart, nk, bias, p=p, dil=dil):
                qrows = pl.ds(qstart, BLK, stride=dil)
                krows = pl.ds(kstart, nk, stride=dil)
                qb = qs[qrows, :].astype(BF16)
                kb = ks[krows, :].astype(BF16)
                vb = vs[krows, :].astype(BF16)
                s = lax.dot_general(qb, kb, (((1,), (1,)), ((), ())), preferred_element_type=F32)
                logits = jnp.where(bias > 0.1 * NEG, s + bias, NEG)
                m = jnp.max(logits, axis=-1, keepdims=True)
                pe = jnp.exp(logits - m)
                acc_s[p, qrows, :] = jnp.dot(pe.astype(BF16), vb, preferred_element_type=F32)
                m_s[p, qrows, :] = m
                l_s[p, qrows, :] = jnp.sum(pe, axis=-1, keepdims=True)

            def first_block(r, carry, block=block, p=p, hh=hh):
                block(r, r, BLK, bias_ref[p, hh, :, BLK:])
                return carry

            def later_block(i, carry, block=block, p=p, hh=hh, dil=dil, shift=shift):
                r = jnp.bitwise_and(i, dil - 1)
                n = jnp.right_shift(i, shift)
                qstart = r + n * (dil * BLK)
                block(qstart, qstart - dil * BLK, 2 * BLK, bias_ref[p, hh])
                return carry

            lax.fori_loop(0, dil, first_block, 0)
            if nblk > 1:
                lax.fori_loop(dil, dil * nblk, later_block, 0)

        ms = [m_s[p] for p in range(len(DILATED))]
        m_all = functools.reduce(jnp.maximum, ms)
        es = [jnp.exp(m - m_all) for m in ms]
        num = sum(e * acc_s[p] for p, e in enumerate(es))
        den = sum(e * l_s[p] for p, e in enumerate(es))
        o_ref[:, lanes] = num / den


def _attention(qkv3, q_norm_w, k_norm_w, bias):
    b, s, _ = qkv3.shape
    nhp = D_ATTN // LANES
    npat = len(DILATED)
    col = lambda off: pl.BlockSpec((None, s, LANES), lambda bi, hp, off=off: (bi, 0, off + hp))
    return pl.pallas_call(
        _attn_kernel,
        grid=(b, nhp),
        in_specs=[col(0), col(nhp), col(2 * nhp),
                  pl.BlockSpec((1, HEAD_DIM), lambda bi, hp: (0, 0)),
                  pl.BlockSpec((1, HEAD_DIM), lambda bi, hp: (0, 0)),
                  pl.BlockSpec((npat, LANES // HEAD_DIM, BLK, 2 * BLK), lambda bi, hp: (0, hp, 0, 0))],
        out_specs=pl.BlockSpec((None, s, LANES), lambda bi, hp: (bi, 0, hp)),
        out_shape=jax.ShapeDtypeStruct((b, s, D_ATTN), F32),
        scratch_shapes=[pltpu.VMEM((s, HEAD_DIM), F32), pltpu.VMEM((s, HEAD_DIM), F32),
                        pltpu.VMEM((s, HEAD_DIM), F32),
                        pltpu.VMEM((npat, s, HEAD_DIM), F32),
                        pltpu.VMEM((npat, s, 1), F32), pltpu.VMEM((npat, s, 1), F32)],
        compiler_params=pltpu.CompilerParams(dimension_semantics=("arbitrary", "arbitrary"),
                                             vmem_limit_bytes=VMEM_LIMIT),
        name="attn",
    )(qkv3, qkv3, qkv3, q_norm_w, k_norm_w, bias)


def _out_proj_kernel(attn_ref, sgu_ref, x_ref, mixw_ref, wout_ref, ln2_ref, x1_ref, hnt_ref):
    an = _rms_rows(attn_ref[...], mixw_ref[...]).astype(BF16)
    y = jnp.dot(an, wout_ref[:D_ATTN, :], preferred_element_type=F32)
    y = y + jnp.dot(sgu_ref[...], wout_ref[D_ATTN:, :], preferred_element_type=F32)
    x1 = x_ref[...] + y
    x1_ref[...] = x1
    hnt_ref[...] = _rms_rows(x1, ln2_ref[...]).T.astype(BF16)


def _out_proj(attn2, sgu_n, x2, mixw_attn, w_out, ln2, tm):
    t, d = x2.shape
    full = lambda shape: pl.BlockSpec(shape, lambda i: (0,) * len(shape))
    return pl.pallas_call(
        _out_proj_kernel,
        grid=(t // tm,),
        in_specs=[pl.BlockSpec((tm, D_ATTN), lambda i: (i, 0)),
                  pl.BlockSpec((tm, D_SGU), lambda i: (i, 0)),
                  pl.BlockSpec((tm, d), lambda i: (i, 0)),
                  full((1, D_ATTN)), full((D_ATTN + D_SGU, d)), full((1, d))],
        out_specs=[pl.BlockSpec((tm, d), lambda i: (i, 0)),
                   pl.BlockSpec((d, tm), lambda i: (0, i))],
        out_shape=[jax.ShapeDtypeStruct((t, d), F32), jax.ShapeDtypeStruct((d, t), BF16)],
        compiler_params=pltpu.CompilerParams(dimension_semantics=("arbitrary",),
                                             vmem_limit_bytes=VMEM_LIMIT),
        name="out_proj",
    )(attn2, sgu_n, x2, mixw_attn, w_out, ln2)


def _top16(s):
    iota = lax.broadcasted_iota(jnp.int32, s.shape, 0)
    rank = jnp.full(s.shape, PEER_TOPK, jnp.int32)
    vals = []
    cur = s
    for r in range(PEER_TOPK):
        m = jnp.max(cur, axis=0, keepdims=True)
        idx = jnp.min(jnp.where(cur == m, iota, s.shape[0]), axis=0, keepdims=True)
        sel = iota == idx
        rank = jnp.where(sel, r, rank)
        cur = jnp.where(sel, -jnp.inf, cur)
        vals.append(m)
    return rank, vals


def _staircase_counts(v1, v2, tb):
    v2_all = jnp.concatenate(v2, axis=0)
    cands, poss, sizes = [], [], []
    for r1 in range(PEER_TOPK):
        rows = PEER_TOPK if r1 == 0 else 8
        r2 = lax.broadcasted_iota(jnp.int32, (rows, tb), 0)
        cands.append(jnp.where(r2 < PEER_TOPK // (r1 + 1), v1[r1] + v2_all[:rows], -jnp.inf))
        poss.append(r2 + r1 * PEER_TOPK)
        sizes.append(rows)
    cand = jnp.concatenate(cands, axis=0)
    pos = jnp.concatenate(poss, axis=0)
    cur = cand
    picked = jnp.zeros(cand.shape, F32)
    for _ in range(PEER_TOPK):
        m = jnp.max(cur, axis=0, keepdims=True)
        idx = jnp.min(jnp.where(cur == m, pos, PEER_TOPK * PEER_TOPK), axis=0, keepdims=True)
        sel = pos == idx
        picked = jnp.where(sel, 1.0, picked)
        cur = jnp.where(sel, -jnp.inf, cur)
    z = jnp.sum(picked * jnp.exp(cand - (v1[0] + v2[0])), axis=0, keepdims=True)
    counts, lo = [], 0
    for rows in sizes:
        counts.append(jnp.sum(picked[lo:lo + rows], axis=0, keepdims=True))
        lo += rows
    return counts, z


def _route_kernel(hnt_ref, wqt_ref, k1_ref, k2_ref, c1_ref, q1_ref, c2_ref, p2_ref, qt_s):
    tb = hnt_ref.shape[1]
    half = PEER_D_KEY // 2
    qt_s[...] = jnp.dot(wqt_ref[...], hnt_ref[...], preferred_element_type=F32)

    def head(h, carry):
        base = pl.multiple_of(h * PEER_D_KEY, PEER_D_KEY)
        qa = qt_s[pl.ds(base, half), :].astype(BF16)
        qb = qt_s[pl.ds(base + half, half), :].astype(BF16)
        s1 = jnp.dot(k1_ref[h], qa, preferred_element_type=F32)
        s2 = jnp.dot(k2_ref[h], qb, preferred_element_type=F32)
        rank1, v1 = _top16(s1)
        rank2, v2 = _top16(s2)
        counts, z = _staircase_counts(v1, v2, tb)
        c1 = jnp.zeros(s1.shape, F32)
        for r1 in range(PEER_TOPK):
            c1 = jnp.where(rank1 == r1, counts[r1], c1)
        c1_ref[h] = c1
        q1_ref[h] = jnp.exp(s1 - v1[0]) / z
        c2_ref[h] = rank2.astype(F32).astype(BF16)
        p2_ref[h] = jnp.exp(s2 - v2[0]).astype(BF16)
        return carry

    lax.fori_loop(0, PEER_HEADS, head, 0)


def _route(hnt, wqt, keys1, keys2, tb):
    d, t = hnt.shape
    nq = wqt.shape[0]
    full = lambda shape: pl.BlockSpec(shape, lambda i: (0,) * len(shape))
    code = pl.BlockSpec((PEER_HEADS, PEER_KEYS, tb), lambda i: (0, 0, i))
    code_shape = lambda dt: jax.ShapeDtypeStruct((PEER_HEADS, PEER_KEYS, t), dt)
    return pl.pallas_call(
        _route_kernel,
        grid=(t // tb,),
        in_specs=[pl.BlockSpec((d, tb), lambda i: (0, i)), full((nq, d)),
                  full(keys1.shape), full(keys2.shape)],
        out_specs=[code] * 4,
        out_shape=[code_shape(F32), code_shape(F32), code_shape(BF16), code_shape(BF16)],
        scratch_shapes=[pltpu.VMEM((nq, tb), F32)],
        compiler_params=pltpu.CompilerParams(dimension_semantics=("arbitrary",),
                                             vmem_limit_bytes=VMEM_LIMIT),
        name="route",
    )(hnt, wqt, keys1, keys2)


def _experts_kernel(hnt_ref, u_ref, vt_ref, c1_ref, q1_ref, c2_ref, p2_ref, x1_ref, o_ref,
                    acc_s, st_s, z_s, hnt_s, c2_s, p2_s, *, n_eb, n_blocks):
    t = pl.program_id(0)
    eb = u_ref.shape[0]
    tb = hnt_ref.shape[1]
    keys_per_block = eb // PEER_KEYS
    j_b = jnp.clip(t - 1, 0, n_blocks - 1) % n_eb
    j_c = jnp.clip(t - 2, 0, n_blocks - 1) % n_eb

    @pl.when(t == 0)
    def _():
        st_s[...] = jnp.zeros_like(st_s)
        z_s[...] = jnp.zeros_like(z_s)
        acc_s[...] = jnp.zeros_like(acc_s)

    @pl.when(jnp.logical_and(t >= 2, j_c == 0))
    def _():
        acc_s[...] = jnp.zeros_like(acc_s)

    @pl.when(jnp.minimum(t, n_blocks - 1) % n_eb == 0)
    def _():
        hnt_s[...] = hnt_ref[...]

    @pl.when(j_b == 0)
    def _():
        c2_s[...] = c2_ref[...]
        p2_s[...] = p2_ref[...]

    def stages(slot):
        row0 = (j_b * keys_per_block) % CODE_ROW_BLOCK
        rows_of = lambda ref, ii: [ref[h, pl.ds(row0 + ii, 1), :] for h in range(PEER_HEADS)]
        c1_all = [rows_of(c1_ref, ii) for ii in range(keys_per_block)]
        q1_all = [rows_of(q1_ref, ii) for ii in range(keys_per_block)]
        packed = (PEER_KEYS // BF16_ROWS, BF16_ROWS, LANES)
        row_bf16 = lambda r: jnp.broadcast_to(r, (BF16_ROWS, LANES)).astype(BF16)[None]

        def stage_a(cc):
            cols = slice(cc * MXU_COLS, (cc + 1) * MXU_COLS)
            st_s[slot, :, cols] = jnp.dot(u_ref[...], hnt_s[:, cols], preferred_element_type=F32)

        def stage_b(ii, cc):
            rows = slice(ii * PEER_KEYS, (ii + 1) * PEER_KEYS)
            cols = slice(cc * LANES, (cc + 1) * LANES)
            gate = None
            for h in range(PEER_HEADS):
                c1 = row_bf16(c1_all[ii][h][:, cols])
                q1 = row_bf16(q1_all[ii][h][:, cols])
                term = jnp.where(c2_s[h, :, cols].reshape(packed) < c1,
                                 p2_s[h, :, cols].reshape(packed) * q1, jnp.zeros((), BF16))
                gate = term if gate is None else gate + term
            act = jax.nn.gelu(st_s[1 - slot, rows, cols]).astype(BF16).reshape(packed)
            z_s[slot, rows, cols] = (act * gate).reshape(PEER_KEYS, LANES)

        def stage_c(cc):
            cols = slice(cc * MXU_COLS, (cc + 1) * MXU_COLS)
            acc_s[:, cols] += jnp.dot(vt_ref[...], z_s[1 - slot, :, cols],
                                      preferred_element_type=F32)

        n_mc = tb // MXU_COLS
        b_tiles = [(ii, cc) for cc in range(tb // LANES) for ii in range(keys_per_block)]
        mxu_chunks = [functools.partial(stage, cc) for cc in range(n_mc) for stage in (stage_a, stage_c)]
        for k, chunk in enumerate(mxu_chunks):
            chunk()
            for tile in b_tiles[k * len(b_tiles) // len(mxu_chunks):
                                (k + 1) * len(b_tiles) // len(mxu_chunks)]:
                stage_b(*tile)

    for parity in range(2):
        pl.when(t % 2 == parity)(functools.partial(stages, parity))

    @pl.when(jnp.logical_and(t >= 2, j_c == n_eb - 1))
    def _():
        o_ref[...] = x1_ref[...] + acc_s[...].T


def _experts(hnt, u_tab, vt_blocks, codes, x1, tb):
    d, t = hnt.shape
    n_eb, _, eb = vt_blocks.shape
    keys_per_block = eb // PEER_KEYS
    assert CODE_ROW_BLOCK % keys_per_block == 0
    n_blocks = (t // tb) * n_eb
    g_a = lambda s: jnp.minimum(s, n_blocks - 1)
    g_b = lambda s: jnp.clip(s - 1, 0, n_blocks - 1)
    g_c = lambda s: jnp.clip(s - 2, 0, n_blocks - 1)
    code_rows = pl.BlockSpec(
        (PEER_HEADS, CODE_ROW_BLOCK, tb),
        lambda s: (0, (g_b(s) % n_eb) * keys_per_block // CODE_ROW_BLOCK, g_b(s) // n_eb))
    code_keys = pl.BlockSpec((PEER_HEADS, PEER_KEYS, tb), lambda s: (0, 0, g_b(s) // n_eb))
    assert codes[0].dtype == F32 and codes[1].dtype == F32
    assert codes[2].dtype == BF16 and codes[3].dtype == BF16
    return pl.pallas_call(
        functools.partial(_experts_kernel, n_eb=n_eb, n_blocks=n_blocks),
        grid=(n_blocks + 2,),
        in_specs=[pl.BlockSpec((d, tb), lambda s: (0, g_a(s) // n_eb)),
                  pl.BlockSpec((eb, d), lambda s: (g_a(s) % n_eb, 0)),
                  pl.BlockSpec((None, d, eb), lambda s: (g_c(s) % n_eb, 0, 0)),
                  code_rows, code_rows, code_keys, code_keys,
                  pl.BlockSpec((tb, d), lambda s: (g_c(s) // n_eb, 0))],
        out_specs=pl.BlockSpec((tb, d), lambda s: (g_c(s) // n_eb, 0)),
        out_shape=jax.ShapeDtypeStruct((t, d), F32),
        scratch_shapes=[pltpu.VMEM((d, tb), F32), pltpu.VMEM((2, eb, tb), F32),
                        pltpu.VMEM((2, eb, tb), BF16), pltpu.VMEM((d, tb), BF16),
                        pltpu.VMEM((PEER_HEADS, PEER_KEYS, tb), BF16),
                        pltpu.VMEM((PEER_HEADS, PEER_KEYS, tb), BF16)],
        compiler_params=pltpu.CompilerParams(dimension_semantics=("arbitrary",),
                                             vmem_limit_bytes=VMEM_LIMIT),
        name="experts",
    )(hnt, u_tab, vt_blocks, *codes, x1)


def kernel(x, ln1_w, w_in, q_norm_w, k_norm_w, sgu_norm_w, sgu_w, sgu_b, mix_norm_w, w_out,
           ln2_w, peer_w_query, peer_keys1, peer_keys2, peer_u, peer_v, rel_bias):
    b, s, d = x.shape
    t = b * s
    bias = _bias_tables(rel_bias)
    for l in range(ln1_w.shape[0]):
        x2 = x.reshape(t, d)
        qkv, sgu_n = _in_proj(x2, ln1_w[l][None], w_in[l].astype(BF16), sgu_norm_w[l], sgu_w[l],
                              sgu_b[l], mix_norm_w[l, D_ATTN:][None], tm=512)
        attn = _attention(qkv.reshape(b, s, 3 * D_ATTN), q_norm_w[l][None], k_norm_w[l][None], bias)
        x1, hnt = _out_proj(attn.reshape(t, D_ATTN), sgu_n, x2, mix_norm_w[l, :D_ATTN][None],
                            w_out[l].astype(BF16), ln2_w[l][None], tm=512)
        codes = _route(hnt, peer_w_query[l].T.astype(BF16), peer_keys1[l].astype(BF16),
                       peer_keys2[l].astype(BF16), tb=256)
        vt_blocks = peer_v[l].reshape(-1, EXPERT_BLOCK, d).transpose(0, 2, 1).astype(BF16)
        out = _experts(hnt, peer_u[l].astype(BF16), vt_blocks, codes, x1, tb=EXPERT_TOKENS)
        x = out.reshape(b, s, d)
    return x
```

```python
import functools
import math

import numpy as np
import jax
import jax.numpy as jnp
from jax import lax
from jax.experimental import pallas as pl
from jax.experimental.pallas import tpu as pltpu

F32 = jnp.float32
BF16 = jnp.bfloat16

ATTN_HEADS = 8
HEAD_DIM = 64
D_ATTN = ATTN_HEADS * HEAD_DIM
SGU_GROUPS = 4
SGU_CH = 128
D_SGU = SGU_GROUPS * SGU_CH
CHUNK = 128
DILATED = ((128, 1), (512, 4), (2048, 16))
BLK = 128
NUM_BUCKETS = 32
MAX_DISTANCE = 2048
PEER_HEADS = 8
PEER_KEYS = 128
PEER_D_KEY = 256
PEER_TOPK = 16
EPS = 1e-6
NEG = -1e30

LANES = 128
BF16_ROWS = 16
MXU_COLS = 256
CODE_ROW_BLOCK = 8
EXPERT_BLOCK = 1024
EXPERT_TOKENS = 512
VMEM_LIMIT = 56 * 1024 * 1024


def _rms_rows(x, w):
    return x * lax.rsqrt(jnp.mean(x * x, axis=-1, keepdims=True) + EPS) * w


def _bucket_tables():
    qi = np.arange(BLK)[:, None]
    kj = np.arange(2 * BLK)[None, :]
    rel = BLK + qi - kj
    max_exact = NUM_BUCKETS // 2
    buckets, valid = [], []
    for window, dil in DILATED:
        w_sub = window // dil
        dist = np.maximum(rel, 0) * dil
        d_f = np.maximum(dist, max_exact).astype(np.float32)
        large = max_exact + (np.log(d_f / np.float32(max_exact)) / np.float32(math.log(MAX_DISTANCE / max_exact))
                             * np.float32(NUM_BUCKETS - max_exact)).astype(np.int32)
        large = np.minimum(large, NUM_BUCKETS - 1)
        buckets.append(np.where(dist < max_exact, dist, large).astype(np.int32))
        valid.append(((rel >= 0) & (rel <= w_sub)).astype(np.int32))
    return np.stack(buckets), np.stack(valid)


def _bias_kernel(rb_ref, bkt_ref, valid_ref, o_ref):
    h = pl.program_id(1)
    bkt = bkt_ref[...]
    acc = jnp.zeros(bkt.shape, F32)
    for j in range(NUM_BUCKETS):
        acc = jnp.where(bkt == j, rb_ref[j, h], acc)
    o_ref[...] = jnp.where(valid_ref[...] > 0, acc, NEG)


def _bias_tables(rel_bias):
    bkt, valid = _bucket_tables()
    npat = len(DILATED)
    blk = pl.BlockSpec((None, BLK, 2 * BLK), lambda p, h: (p, 0, 0))
    return pl.pallas_call(
        _bias_kernel,
        grid=(npat, ATTN_HEADS),
        in_specs=[pl.BlockSpec(memory_space=pltpu.SMEM), blk, blk],
        out_specs=pl.BlockSpec((None, None, BLK, 2 * BLK), lambda p, h: (p, h, 0, 0)),
        out_shape=jax.ShapeDtypeStruct((npat, ATTN_HEADS, BLK, 2 * BLK), F32),
        name="bias",
    )(rel_bias, jnp.asarray(bkt), jnp.asarray(valid))


def _in_proj_kernel(x_ref, ln1_ref, win_ref, sgnw_ref, sguw_ref, bst_ref, mixw_ref,
                    qkv_ref, sgu_ref, sgu_s):
    tm = x_ref.shape[0]
    h = _rms_rows(x_ref[...], ln1_ref[...]).astype(BF16)
    proj = jnp.dot(h, win_ref[...], preferred_element_type=F32)
    qkv_ref[...] = proj[:, :3 * D_ATTN]
    row = lax.broadcasted_iota(jnp.int32, (CHUNK, CHUNK), 0)
    col = lax.broadcasted_iota(jnp.int32, (CHUNK, CHUNK), 1)
    for g in range(SGU_GROUPS):
        u_lo = 3 * D_ATTN + g * SGU_CH
        v_lo = 3 * D_ATTN + D_SGU + g * SGU_CH
        ua = jax.nn.gelu(proj[:, u_lo:u_lo + SGU_CH])
        va = jax.nn.gelu(proj[:, v_lo:v_lo + SGU_CH])
        vn = _rms_rows(va, sgnw_ref[g:g + 1, :]).astype(BF16)
        w = jnp.where(row >= col, sguw_ref[g], 0.0).astype(BF16)
        bias = bst_ref[:, g:g + 1]
        for c in range(tm // CHUNK):
            rows = slice(c * CHUNK, (c + 1) * CHUNK)
            spatial = jnp.dot(w, vn[rows], preferred_element_type=F32) + bias
            sgu_s[rows, g * SGU_CH:(g + 1) * SGU_CH] = ua[rows] * spatial
    sgu_ref[...] = _rms_rows(sgu_s[...], mixw_ref[...]).astype(BF16)


def _in_proj(x2, ln1, w_in, sgu_norm_w, sgu_w, sgu_b, mixw_sgu, tm):
    t, d = x2.shape
    dproj = w_in.shape[1]
    full = lambda shape: pl.BlockSpec(shape, lambda i: (0,) * len(shape))
    return pl.pallas_call(
        _in_proj_kernel,
        grid=(t // tm,),
        in_specs=[pl.BlockSpec((tm, d), lambda i: (i, 0)),
                  full((1, d)), full((d, dproj)), full((SGU_GROUPS, SGU_CH)),
                  full((SGU_GROUPS, CHUNK, CHUNK)), full((CHUNK, SGU_GROUPS)), full((1, D_SGU))],
        out_specs=[pl.BlockSpec((tm, 3 * D_ATTN), lambda i: (i, 0)),
                   pl.BlockSpec((tm, D_SGU), lambda i: (i, 0))],
        out_shape=[jax.ShapeDtypeStruct((t, 3 * D_ATTN), F32),
                   jax.ShapeDtypeStruct((t, D_SGU), BF16)],
        scratch_shapes=[pltpu.VMEM((tm, D_SGU), F32)],
        compiler_params=pltpu.CompilerParams(dimension_semantics=("arbitrary",),
                                             vmem_limit_bytes=VMEM_LIMIT),
        name="in_proj",
    )(x2, ln1, w_in, sgu_norm_w, sgu_w, sgu_b.T, mixw_sgu)


def _attn_kernel(q_ref, k_ref, v_ref, qw_ref, kw_ref, bias_ref, o_ref,
                 qs, ks, vs, acc_s, m_s, l_s):
    s_len = q_ref.shape[0]
    for hh in range(LANES // HEAD_DIM):
        lanes = slice(hh * HEAD_DIM, (hh + 1) * HEAD_DIM)
        qs[...] = _rms_rows(q_ref[:, lanes], qw_ref[...]) * (HEAD_DIM ** -0.5)
        ks[...] = _rms_rows(k_ref[:, lanes], kw_ref[...])
        vs[...] = v_ref[:, lanes]

        for p, (_, dil) in enumerate(DILATED):
            nblk = s_len // dil // BLK
            shift = dil.bit_length() - 1

            def block(qstart, kstart, nk, bias, p=p, dil=dil):
                qrows = pl.ds(qstart, BLK, stride=dil)
                krows = pl.ds(kstart, nk, stride=dil)
                qb = qs[qrows, :].astype(BF16)
                kb = ks[krows, :].astype(BF16)
                vb = vs[krows, :].astype(BF16)
                s = lax.dot_general(qb, kb, (((1,), (1,)), ((), ())), preferred_element_type=F32)
                logits = jnp.where(bias > 0.1 * NEG, s + bias, NEG)
                m = jnp.max(logits, axis=-1, keepdims=True)
                pe = jnp.exp(logits - m)
                acc_s[p, qrows, :] = jnp.dot(pe.astype(BF16), vb, preferred_element_type=F32)
                m_s[p, qrows, :] = m
                l_s[p, qrows, :] = jnp.sum(pe, axis=-1, keepdims=True)

            def first_block(r, carry, block=block, p=p, hh=hh):
                block(r, r, BLK, bias_ref[p, hh, :, BLK:])
                return carry

            def later_block(i, carry, block=block, p=p, hh=hh, dil=dil, shift=shift):
                r = jnp.bitwise_and(i, dil - 1)
                n = jnp.right_shift(i, shift)
                qstart = r + n * (dil * BLK)
                block(qstart, qstart - dil * BLK, 2 * BLK, bias_ref[p, hh])
                return carry

            lax.fori_loop(0, dil, first_block, 0)
            if nblk > 1:
                lax.fori_loop(dil, dil * nblk, later_block, 0)

        ms = [m_s[p] for p in range(len(DILATED))]
        m_all = functools.reduce(jnp.maximum, ms)
        es = [jnp.exp(m - m_all) for m in ms]
        num = sum(e * acc_s[p] for p, e in enumerate(es))
        den = sum(e * l_s[p] for p, e in enumerate(es))
        o_ref[:, lanes] = num / den


def _attention(qkv3, q_norm_w, k_norm_w, bias):
    b, s, _ = qkv3.shape
    nhp = D_ATTN // LANES
    npat = len(DILATED)
    col = lambda off: pl.BlockSpec((None, s, LANES), lambda bi, hp, off=off: (bi, 0, off + hp))
    return pl.pallas_call(
        _attn_kernel,
        grid=(b, nhp),
        in_specs=[col(0), col(nhp), col(2 * nhp),
                  pl.BlockSpec((1, HEAD_DIM), lambda bi, hp: (0, 0)),
                  pl.BlockSpec((1, HEAD_DIM), lambda bi, hp: (0, 0)),
                  pl.BlockSpec((npat, LANES // HEAD_DIM, BLK, 2 * BLK), lambda bi, hp: (0, hp, 0, 0))],
        out_specs=pl.BlockSpec((None, s, LANES), lambda bi, hp: (bi, 0, hp)),
        out_shape=jax.ShapeDtypeStruct((b, s, D_ATTN), F32),
        scratch_shapes=[pltpu.VMEM((s, HEAD_DIM), F32), pltpu.VMEM((s, HEAD_DIM), F32),
                        pltpu.VMEM((s, HEAD_DIM), F32),
                        pltpu.VMEM((npat, s, HEAD_DIM), F32),
                        pltpu.VMEM((npat, s, 1), F32), pltpu.VMEM((npat, s, 1), F32)],
        compiler_params=pltpu.CompilerParams(dimension_semantics=("arbitrary", "arbitrary"),
                                             vmem_limit_bytes=VMEM_LIMIT),
        name="attn",
    )(qkv3, qkv3, qkv3, q_norm_w, k_norm_w, bias)


def _out_proj_kernel(attn_ref, sgu_ref, x_ref, mixw_ref, wout_ref, ln2_ref, x1_ref, hnt_ref):
    an = _rms_rows(attn_ref[...], mixw_ref[...]).astype(BF16)
    y = jnp.dot(an, wout_ref[:D_ATTN, :], preferred_element_type=F32)
    y = y + jnp.dot(sgu_ref[...], wout_ref[D_ATTN:, :], preferred_element_type=F32)
    x1 = x_ref[...] + y
    x1_ref[...] = x1
    hnt_ref[...] = _rms_rows(x1, ln2_ref[...]).T.astype(BF16)


def _out_proj(attn2, sgu_n, x2, mixw_attn, w_out, ln2, tm):
    t, d = x2.shape
    full = lambda shape: pl.BlockSpec(shape, lambda i: (0,) * len(shape))
    return pl.pallas_call(
        _out_proj_kernel,
        grid=(t // tm,),
        in_specs=[pl.BlockSpec((tm, D_ATTN), lambda i: (i, 0)),
                  pl.BlockSpec((tm, D_SGU), lambda i: (i, 0)),
                  pl.BlockSpec((tm, d), lambda i: (i, 0)),
                  full((1, D_ATTN)), full((D_ATTN + D_SGU, d)), full((1, d))],
        out_specs=[pl.BlockSpec((tm, d), lambda i: (i, 0)),
                   pl.BlockSpec((d, tm), lambda i: (0, i))],
        out_shape=[jax.ShapeDtypeStruct((t, d), F32), jax.ShapeDtypeStruct((d, t), BF16)],
        compiler_params=pltpu.CompilerParams(dimension_semantics=("arbitrary",),
                                             vmem_limit_bytes=VMEM_LIMIT),
        name="out_proj",
    )(attn2, sgu_n, x2, mixw_attn, w_out, ln2)


def _top16(s):
    iota = lax.broadcasted_iota(jnp.int32, s.shape, 0)
    rank = jnp.full(s.shape, PEER_TOPK, jnp.int32)
    vals = []
    cur = s
    for r in range(PEER_TOPK):
        m = jnp.max(cur, axis=0, keepdims=True)
        idx = jnp.min(jnp.where(cur == m, iota, s.shape[0]), axis=0, keepdims=True)
        sel = iota == idx
        rank = jnp.where(sel, r, rank)
        cur = jnp.where(sel, -jnp.inf, cur)
        vals.append(m)
    return rank, vals


def _staircase_counts(v1, v2, tb):
    v2_all = jnp.concatenate(v2, axis=0)
    cands, poss, sizes = [], [], []
    for r1 in range(PEER_TOPK):
        rows = PEER_TOPK if r1 == 0 else 8
        r2 = lax.broadcasted_iota(jnp.int32, (rows, tb), 0)
        cands.append(jnp.where(r2 < PEER_TOPK // (r1 + 1), v1[r1] + v2_all[:rows], -jnp.inf))
        poss.append(r2 + r1 * PEER_TOPK)
        sizes.append(rows)
    cand = jnp.concatenate(cands, axis=0)
    pos = jnp.concatenate(poss, axis=0)
    cur = cand
    picked = jnp.zeros(cand.shape, F32)
    for _ in range(PEER_TOPK):
        m = jnp.max(cur, axis=0, keepdims=True)
        idx = jnp.min(jnp.where(cur == m, pos, PEER_TOPK * PEER_TOPK), axis=0, keepdims=True)
        sel = pos == idx
        picked = jnp.where(sel, 1.0, picked)
        cur = jnp.where(sel, -jnp.inf, cur)
    z = jnp.sum(picked * jnp.exp(cand - (v1[0] + v2[0])), axis=0, keepdims=True)
    counts, lo = [], 0
    for rows in sizes:
        counts.append(jnp.sum(picked[lo:lo + rows], axis=0, keepdims=True))
        lo += rows
    return counts, z


def _route_kernel(hnt_ref, wqt_ref, k1_ref, k2_ref, c1_ref, q1_ref, c2_ref, p2_ref, qt_s):
    tb = hnt_ref.shape[1]
    half = PEER_D_KEY // 2
    qt_s[...] = jnp.dot(wqt_ref[...], hnt_ref[...], preferred_element_type=F32)

    def head(h, carry):
        base = pl.multiple_of(h * PEER_D_KEY, PEER_D_KEY)
        qa = qt_s[pl.ds(base, half), :].astype(BF16)
        qb = qt_s[pl.ds(base + half, half), :].astype(BF16)
        s1 = jnp.dot(k1_ref[h], qa, preferred_element_type=F32)
        s2 = jnp.dot(k2_ref[h], qb, preferred_element_type=F32)
        rank1, v1 = _top16(s1)
        rank2, v2 = _top16(s2)
        counts, z = _staircase_counts(v1, v2, tb)
        c1 = jnp.zeros(s1.shape, F32)
        for r1 in range(PEER_TOPK):
            c1 = jnp.where(rank1 == r1, counts[r1], c1)
        c1_ref[h] = c1
        q1_ref[h] = jnp.exp(s1 - v1[0]) / z
        c2_ref[h] = rank2.astype(F32).astype(BF16)
        p2_ref[h] = jnp.exp(s2 - v2[0]).astype(BF16)
        return carry

    lax.fori_loop(0, PEER_HEADS, head, 0)


def _route(hnt, wqt, keys1, keys2, tb):
    d, t = hnt.shape
    nq = wqt.shape[0]
    full = lambda shape: pl.BlockSpec(shape, lambda i: (0,) * len(shape))
    code = pl.BlockSpec((PEER_HEADS, PEER_KEYS, tb), lambda i: (0, 0, i))
    code_shape = lambda dt: jax.ShapeDtypeStruct((PEER_HEADS, PEER_KEYS, t), dt)
    return pl.pallas_call(
        _route_kernel,
        grid=(t // tb,),
        in_specs=[pl.BlockSpec((d, tb), lambda i: (0, i)), full((nq, d)),
                  full(keys1.shape), full(keys2.shape)],
        out_specs=[code] * 4,
        out_shape=[code_shape(F32), code_shape(F32), code_shape(BF16), code_shape(BF16)],
        scratch_shapes=[pltpu.VMEM((nq, tb), F32)],
        compiler_params=pltpu.CompilerParams(dimension_semantics=("arbitrary",),
                                             vmem_limit_bytes=VMEM_LIMIT),
        name="route",
    )(hnt, wqt, keys1, keys2)


def _experts_kernel(hnt_ref, u_ref, vt_ref, c1_ref, q1_ref, c2_ref, p2_ref, x1_ref, o_ref,
                    acc_s, st_s, z_s, hnt_s, c2_s, p2_s, *, n_eb, n_blocks):
    t = pl.program_id(0)
    eb = u_ref.shape[0]
    tb = hnt_ref.shape[1]
    keys_per_block = eb // PEER_KEYS
    j_b = jnp.clip(t - 1, 0, n_blocks - 1) % n_eb
    j_c = jnp.clip(t - 2, 0, n_blocks - 1) % n_eb

    @pl.when(t == 0)
    def _():
        st_s[...] = jnp.zeros_like(st_s)
        z_s[...] = jnp.zeros_like(z_s)
        acc_s[...] = jnp.zeros_like(acc_s)

    @pl.when(jnp.logical_and(t >= 2, j_c == 0))
    def _():
        acc_s[...] = jnp.zeros_like(acc_s)

    @pl.when(jnp.minimum(t, n_blocks - 1) % n_eb == 0)
    def _():
        hnt_s[...] = hnt_ref[...]

    @pl.when(j_b == 0)
    def _():
        c2_s[...] = c2_ref[...]
        p2_s[...] = p2_ref[...]

    row0 = (j_b * keys_per_block) % CODE_ROW_BLOCK
    rows_of = lambda ref, ii: [ref[h, pl.ds(row0 + ii, 1), :] for h in range(PEER_HEADS)]
    c1_all = [rows_of(c1_ref, ii) for ii in range(keys_per_block)]
    q1_all = [rows_of(q1_ref, ii) for ii in range(keys_per_block)]
    packed = (PEER_KEYS // BF16_ROWS, BF16_ROWS, LANES)
    row_bf16 = lambda r: jnp.broadcast_to(r, (BF16_ROWS, LANES)).astype(BF16)[None]

    def stage_a(cols):
        st_s[:, cols] = jnp.dot(u_ref[...], hnt_s[:, cols], preferred_element_type=F32)

    def stage_b(ii, cols):
        rows = slice(ii * PEER_KEYS, (ii + 1) * PEER_KEYS)
        gate = None
        for h in range(PEER_HEADS):
            c1 = row_bf16(c1_all[ii][h][:, cols])
            q1 = row_bf16(q1_all[ii][h][:, cols])
            term = jnp.where(c2_s[h, :, cols].reshape(packed) < c1,
                             p2_s[h, :, cols].reshape(packed) * q1, jnp.zeros((), BF16))
            gate = term if gate is None else gate + term
        act = jax.nn.gelu(st_s[rows, cols]).astype(BF16).reshape(packed)
        z_s[rows, cols] = (act * gate).reshape(PEER_KEYS, LANES)

    def stage_c(cols):
        acc_s[:, cols] += jnp.dot(vt_ref[...], z_s[:, cols], preferred_element_type=F32)

    for mc in range(tb // MXU_COLS):
        stage_c(slice(mc * MXU_COLS, (mc + 1) * MXU_COLS))
        for lc in range(MXU_COLS // LANES):
            lo = mc * MXU_COLS + lc * LANES
            for ii in range(keys_per_block):
                stage_b(ii, slice(lo, lo + LANES))
        stage_a(slice(mc * MXU_COLS, (mc + 1) * MXU_COLS))

    @pl.when(jnp.logical_and(t >= 2, j_c == n_eb - 1))
    def _():
        o_ref[...] = x1_ref[...] + acc_s[...].T


def _experts(hnt, u_tab, vt_blocks, codes, x1, tb):
    d, t = hnt.shape
    n_eb, _, eb = vt_blocks.shape
    keys_per_block = eb // PEER_KEYS
    assert CODE_ROW_BLOCK % keys_per_block == 0
    n_blocks = (t // tb) * n_eb
    g_a = lambda s: jnp.minimum(s, n_blocks - 1)
    g_b = lambda s: jnp.clip(s - 1, 0, n_blocks - 1)
    g_c = lambda s: jnp.clip(s - 2, 0, n_blocks - 1)
    code_rows = pl.BlockSpec(
        (PEER_HEADS, CODE_ROW_BLOCK, tb),
        lambda s: (0, (g_b(s) % n_eb) * keys_per_block // CODE_ROW_BLOCK, g_b(s) // n_eb))
    code_keys = pl.BlockSpec((PEER_HEADS, PEER_KEYS, tb), lambda s: (0, 0, g_b(s) // n_eb))
    assert codes[0].dtype == F32 and codes[1].dtype == F32
    assert codes[2].dtype == BF16 and codes[3].dtype == BF16
    return pl.pallas_call(
        functools.partial(_experts_kernel, n_eb=n_eb, n_blocks=n_blocks),
        grid=(n_blocks + 2,),
        in_specs=[pl.BlockSpec((d, tb), lambda s: (0, g_a(s) // n_eb)),
                  pl.BlockSpec((eb, d), lambda s: (g_a(s) % n_eb, 0)),
                  pl.BlockSpec((None, d, eb), lambda s: (g_c(s) % n_eb, 0, 0)),
                  code_rows, code_rows, code_keys, code_keys,
                  pl.BlockSpec((tb, d), lambda s: (g_c(s) // n_eb, 0))],
        out_specs=pl.BlockSpec((tb, d), lambda s: (g_c(s) // n_eb, 0)),
        out_shape=jax.ShapeDtypeStruct((t, d), F32),
        scratch_shapes=[pltpu.VMEM((d, tb), F32), pltpu.VMEM((eb, tb), F32),
                        pltpu.VMEM((eb, tb), BF16), pltpu.VMEM((d, tb), BF16),
                        pltpu.VMEM((PEER_HEADS, PEER_KEYS, tb), BF16),
                        pltpu.VMEM((PEER_HEADS, PEER_KEYS, tb), BF16)],
        compiler_params=pltpu.CompilerParams(dimension_semantics=("arbitrary",),
                                             vmem_limit_bytes=VMEM_LIMIT),
        name="experts",
    )(hnt, u_tab, vt_blocks, *codes, x1)


def kernel(x, ln1_w, w_in, q_norm_w, k_norm_w, sgu_norm_w, sgu_w, sgu_b, mix_norm_w, w_out,
           ln2_w, peer_w_query, peer_keys1, peer_keys2, peer_u, peer_v, rel_bias):
    b, s, d = x.shape
    t = b * s
    bias = _bias_tables(rel_bias)
    for l in range(ln1_w.shape[0]):
        x2 = x.reshape(t, d)
        qkv, sgu_n = _in_proj(x2, ln1_w[l][None], w_in[l].astype(BF16), sgu_norm_w[l], sgu_w[l],
                              sgu_b[l], mix_norm_w[l, D_ATTN:][None], tm=512)
        attn = _attention(qkv.reshape(b, s, 3 * D_ATTN), q_norm_w[l][None], k_norm_w[l][None], bias)
        x1, hnt = _out_proj(attn.reshape(t, D_ATTN), sgu_n, x2, mix_norm_w[l, :D_ATTN][None],
                            w_out[l].astype(BF16), ln2_w[l][None], tm=512)
        codes = _route(hnt, peer_w_query[l].T.astype(BF16), peer_keys1[l].astype(BF16),
                       peer_keys2[l].astype(BF16), tb=256)
        vt_blocks = peer_v[l].reshape(-1, EXPERT_BLOCK, d).transpose(0, 2, 1).astype(BF16)
        out = _experts(hnt, peer_u[l].astype(BF16), vt_blocks, codes, x1, tb=EXPERT_TOKENS)
        x = out.reshape(b, s, d)
    return x
```

```python
import functools
import math

import numpy as np
import jax
import jax.numpy as jnp
from jax import lax
from jax.experimental import pallas as pl
from jax.experimental.pallas import tpu as pltpu

F32 = jnp.float32
BF16 = jnp.bfloat16

ATTN_HEADS = 8
HEAD_DIM = 64
D_ATTN = ATTN_HEADS * HEAD_DIM
SGU_GROUPS = 4
SGU_CH = 128
D_SGU = SGU_GROUPS * SGU_CH
CHUNK = 128
DILATED = ((128, 1), (512, 4), (2048, 16))
BLK = 128
NUM_BUCKETS = 32
MAX_DISTANCE = 2048
PEER_HEADS = 8
PEER_KEYS = 128
PEER_D_KEY = 256
PEER_TOPK = 16
EPS = 1e-6
NEG = -1e30

LANES = 128
BF16_ROWS = 16
MXU_COLS = 256
CODE_ROW_BLOCK = 8
ATTN_GROUP = 4
EXPERT_BLOCK = 1024
EXPERT_TOKENS = 512
VMEM_LIMIT = 56 * 1024 * 1024


def _rms_rows(x, w):
    return x * lax.rsqrt(jnp.mean(x * x, axis=-1, keepdims=True) + EPS) * w


def _bucket_tables():
    qi = np.arange(BLK)[:, None]
    kj = np.arange(2 * BLK)[None, :]
    rel = BLK + qi - kj
    max_exact = NUM_BUCKETS // 2
    buckets, valid = [], []
    for window, dil in DILATED:
        w_sub = window // dil
        dist = np.maximum(rel, 0) * dil
        d_f = np.maximum(dist, max_exact).astype(np.float32)
        large = max_exact + (np.log(d_f / np.float32(max_exact)) / np.float32(math.log(MAX_DISTANCE / max_exact))
                             * np.float32(NUM_BUCKETS - max_exact)).astype(np.int32)
        large = np.minimum(large, NUM_BUCKETS - 1)
        buckets.append(np.where(dist < max_exact, dist, large).astype(np.int32))
        valid.append(((rel >= 0) & (rel <= w_sub)).astype(np.int32))
    return np.stack(buckets), np.stack(valid)


def _bias_kernel(rb_ref, bkt_ref, valid_ref, o_ref):
    h = pl.program_id(1)
    bkt = bkt_ref[...]
    acc = jnp.zeros(bkt.shape, F32)
    for j in range(NUM_BUCKETS):
        acc = jnp.where(bkt == j, rb_ref[j, h], acc)
    o_ref[...] = jnp.where(valid_ref[...] > 0, acc, NEG)


def _bias_tables(rel_bias):
    bkt, valid = _bucket_tables()
    npat = len(DILATED)
    blk = pl.BlockSpec((None, BLK, 2 * BLK), lambda p, h: (p, 0, 0))
    return pl.pallas_call(
        _bias_kernel,
        grid=(npat, ATTN_HEADS),
        in_specs=[pl.BlockSpec(memory_space=pltpu.SMEM), blk, blk],
        out_specs=pl.BlockSpec((None, None, BLK, 2 * BLK), lambda p, h: (p, h, 0, 0)),
        out_shape=jax.ShapeDtypeStruct((npat, ATTN_HEADS, BLK, 2 * BLK), F32),
        name="bias",
    )(rel_bias, jnp.asarray(bkt), jnp.asarray(valid))


def _in_proj_kernel(x_ref, ln1_ref, win_ref, sgnw_ref, sguw_ref, bst_ref, mixw_ref,
                    qkv_ref, sgu_ref, sgu_s):
    tm = x_ref.shape[0]
    h = _rms_rows(x_ref[...], ln1_ref[...]).astype(BF16)
    proj = jnp.dot(h, win_ref[...], preferred_element_type=F32)
    qkv_ref[...] = proj[:, :3 * D_ATTN]
    row = lax.broadcasted_iota(jnp.int32, (CHUNK, CHUNK), 0)
    col = lax.broadcasted_iota(jnp.int32, (CHUNK, CHUNK), 1)
    for g in range(SGU_GROUPS):
        u_lo = 3 * D_ATTN + g * SGU_CH
        v_lo = 3 * D_ATTN + D_SGU + g * SGU_CH
        ua = jax.nn.gelu(proj[:, u_lo:u_lo + SGU_CH])
        va = jax.nn.gelu(proj[:, v_lo:v_lo + SGU_CH])
        vn = _rms_rows(va, sgnw_ref[g:g + 1, :]).astype(BF16)
        w = jnp.where(row >= col, sguw_ref[g], 0.0).astype(BF16)
        bias = bst_ref[:, g:g + 1]
        for c in range(tm // CHUNK):
            rows = slice(c * CHUNK, (c + 1) * CHUNK)
            spatial = jnp.dot(w, vn[rows], preferred_element_type=F32) + bias
            sgu_s[rows, g * SGU_CH:(g + 1) * SGU_CH] = ua[rows] * spatial
    sgu_ref[...] = _rms_rows(sgu_s[...], mixw_ref[...]).astype(BF16)


def _in_proj(x2, ln1, w_in, sgu_norm_w, sgu_w, sgu_b, mixw_sgu, tm):
    t, d = x2.shape
    dproj = w_in.shape[1]
    full = lambda shape: pl.BlockSpec(shape, lambda i: (0,) * len(shape))
    return pl.pallas_call(
        _in_proj_kernel,
        grid=(t // tm,),
        in_specs=[pl.BlockSpec((tm, d), lambda i: (i, 0)),
                  full((1, d)), full((d, dproj)), full((SGU_GROUPS, SGU_CH)),
                  full((SGU_GROUPS, CHUNK, CHUNK)), full((CHUNK, SGU_GROUPS)), full((1, D_SGU))],
        out_specs=[pl.BlockSpec((tm, 3 * D_ATTN), lambda i: (i, 0)),
                   pl.BlockSpec((tm, D_SGU), lambda i: (i, 0))],
        out_shape=[jax.ShapeDtypeStruct((t, 3 * D_ATTN), F32),
                   jax.ShapeDtypeStruct((t, D_SGU), BF16)],
        scratch_shapes=[pltpu.VMEM((tm, D_SGU), F32)],
        compiler_params=pltpu.CompilerParams(dimension_semantics=("arbitrary",),
                                             vmem_limit_bytes=VMEM_LIMIT),
        name="in_proj",
    )(x2, ln1, w_in, sgu_norm_w, sgu_w, sgu_b.T, mixw_sgu)


def _attn_kernel(q_ref, k_ref, v_ref, qw_ref, kw_ref, bias_ref, o_ref,
                 qs, ks, vs, acc_s, m_s):
    s_len = q_ref.shape[0]
    heads = LANES // HEAD_DIM
    lane = lax.broadcasted_iota(jnp.int32, (s_len, LANES), 1)
    r_i = lax.broadcasted_iota(jnp.int32, (LANES, LANES), 0) // HEAD_DIM
    c_i = lax.broadcasted_iota(jnp.int32, (LANES, LANES), 1) // HEAD_DIM
    averager = jnp.where(r_i == c_i, 1.0 / HEAD_DIM, 0.0).astype(BF16)

    def head_rms(x, w):
        sq = x * x
        hi = sq.astype(BF16)
        lo = (sq - hi.astype(F32)).astype(BF16)
        mean = (jnp.dot(hi, averager, preferred_element_type=F32)
                + jnp.dot(lo, averager, preferred_element_type=F32))
        return x * lax.rsqrt(mean + EPS) * w

    qn = head_rms(q_ref[...], qw_ref[...]) * (HEAD_DIM ** -0.5)
    ks[...] = head_rms(k_ref[...], kw_ref[...])
    v = v_ref[...]
    for hh in range(heads):
        mine = (lane // HEAD_DIM) == hh
        qs[hh] = jnp.where(mine, qn, 0.0)
        vs[hh] = jnp.where(mine, v, 1.0)

    for p, (_, dil) in enumerate(DILATED):
        nblk = s_len // dil // BLK
        shift = dil.bit_length() - 1

        def group(qstarts, first, p=p, dil=dil):
            nk = BLK if first else 2 * BLK
            chains = [(hh, q0) for q0 in qstarts for hh in range(heads)]
            rows = lambda q0, n: pl.ds(q0, n, stride=dil)
            logits = []
            for hh, q0 in chains:
                qb = qs[hh, rows(q0, BLK), :].astype(BF16)
                kb = ks[rows(q0 - (nk - BLK) * dil, nk), :].astype(BF16)
                s = lax.dot_general(qb, kb, (((1,), (1,)), ((), ())), preferred_element_type=F32)
                bias = bias_ref[p, hh, :, BLK:] if first else bias_ref[p, hh]
                logits.append(jnp.where(bias > 0.1 * NEG, s + bias, NEG))
            probs = []
            for (hh, q0), lg in zip(chains, logits):
                m = jnp.max(lg, axis=-1, keepdims=True)
                m_s[hh, p, rows(q0, BLK), :] = jnp.broadcast_to(m, (BLK, LANES))
                probs.append(jnp.exp(lg - m).astype(BF16))
            for (hh, q0), pe in zip(chains, probs):
                vb = vs[hh, rows(q0 - (nk - BLK) * dil, nk), :].astype(BF16)
                acc_s[hh, p, rows(q0, BLK), :] = jnp.dot(pe, vb, preferred_element_type=F32)

        def first_groups(g, carry, group=group):
            group([g * ATTN_GROUP + j for j in range(min(dil, ATTN_GROUP))], True)
            return carry

        def later_groups(g, carry, group=group, dil=dil, shift=shift):
            starts = []
            for j in range(ATTN_GROUP):
                i = dil + g * ATTN_GROUP + j
                starts.append(jnp.bitwise_and(i, dil - 1) + jnp.right_shift(i, shift) * (dil * BLK))
            group(starts, False)
            return carry

        lax.fori_loop(0, -(-dil // ATTN_GROUP), first_groups, 0)
        n_later = (nblk - 1) * dil
        lax.fori_loop(0, n_later // ATTN_GROUP, later_groups, 0)
        tail = [dil + n_later // ATTN_GROUP * ATTN_GROUP + j for j in range(n_later % ATTN_GROUP)]
        if tail:
            group([(i & (dil - 1)) + (i >> shift) * (dil * BLK) for i in tail], False)

    out = None
    for hh in range(heads):
        ms = [m_s[hh, p] for p in range(len(DILATED))]
        m_all = functools.reduce(jnp.maximum, ms)
        tot = sum(jnp.exp(m - m_all) * acc_s[hh, p] for p, m in enumerate(ms))
        ratio = tot / pltpu.roll(tot, HEAD_DIM, axis=1)
        out = ratio if out is None else jnp.where((lane // HEAD_DIM) == hh, ratio, out)
    o_ref[...] = out


def _attention(qkv3, q_norm_w, k_norm_w, bias):
    b, s, _ = qkv3.shape
    nhp = D_ATTN // LANES
    heads = LANES // HEAD_DIM
    npat = len(DILATED)
    col = lambda off: pl.BlockSpec((None, s, LANES), lambda bi, hp, off=off: (bi, 0, off + hp))
    return pl.pallas_call(
        _attn_kernel,
        grid=(b, nhp),
        in_specs=[col(0), col(nhp), col(2 * nhp),
                  pl.BlockSpec((1, LANES), lambda bi, hp: (0, 0)),
                  pl.BlockSpec((1, LANES), lambda bi, hp: (0, 0)),
                  pl.BlockSpec((npat, LANES // HEAD_DIM, BLK, 2 * BLK), lambda bi, hp: (0, hp, 0, 0))],
        out_specs=pl.BlockSpec((None, s, LANES), lambda bi, hp: (bi, 0, hp)),
        out_shape=jax.ShapeDtypeStruct((b, s, D_ATTN), F32),
        scratch_shapes=[pltpu.VMEM((heads, s, LANES), F32), pltpu.VMEM((s, LANES), F32),
                        pltpu.VMEM((heads, s, LANES), F32),
                        pltpu.VMEM((heads, npat, s, LANES), F32),
                        pltpu.VMEM((heads, npat, s, LANES), F32)],
        compiler_params=pltpu.CompilerParams(dimension_semantics=("arbitrary", "arbitrary"),
                                             vmem_limit_bytes=VMEM_LIMIT),
        name="attn",
    )(qkv3, qkv3, qkv3, jnp.tile(q_norm_w, (1, heads)), jnp.tile(k_norm_w, (1, heads)), bias)


def _out_proj_kernel(attn_ref, sgu_ref, x_ref, mixw_ref, wout_ref, ln2_ref, x1_ref, hnt_ref):
    an = _rms_rows(attn_ref[...], mixw_ref[...]).astype(BF16)
    y = jnp.dot(an, wout_ref[:D_ATTN, :], preferred_element_type=F32)
    y = y + jnp.dot(sgu_ref[...], wout_ref[D_ATTN:, :], preferred_element_type=F32)
    x1 = x_ref[...] + y
    x1_ref[...] = x1
    hnt_ref[...] = _rms_rows(x1, ln2_ref[...]).T.astype(BF16)


def _out_proj(attn2, sgu_n, x2, mixw_attn, w_out, ln2, tm):
    t, d = x2.shape
    full = lambda shape: pl.BlockSpec(shape, lambda i: (0,) * len(shape))
    return pl.pallas_call(
        _out_proj_kernel,
        grid=(t // tm,),
        in_specs=[pl.BlockSpec((tm, D_ATTN), lambda i: (i, 0)),
                  pl.BlockSpec((tm, D_SGU), lambda i: (i, 0)),
                  pl.BlockSpec((tm, d), lambda i: (i, 0)),
                  full((1, D_ATTN)), full((D_ATTN + D_SGU, d)), full((1, d))],
        out_specs=[pl.BlockSpec((tm, d), lambda i: (i, 0)),
                   pl.BlockSpec((d, tm), lambda i: (0, i))],
        out_shape=[jax.ShapeDtypeStruct((t, d), F32), jax.ShapeDtypeStruct((d, t), BF16)],
        compiler_params=pltpu.CompilerParams(dimension_semantics=("arbitrary",),
                                             vmem_limit_bytes=VMEM_LIMIT),
        name="out_proj",
    )(attn2, sgu_n, x2, mixw_attn, w_out, ln2)


def _top16(s):
    iota = lax.broadcasted_iota(jnp.int32, s.shape, 0)
    rank = jnp.full(s.shape, PEER_TOPK, jnp.int32)
    vals = []
    cur = s
    for r in range(PEER_TOPK):
        m = jnp.max(cur, axis=0, keepdims=True)
        idx = jnp.min(jnp.where(cur == m, iota, s.shape[0]), axis=0, keepdims=True)
        sel = iota == idx
        rank = jnp.where(sel, r, rank)
        cur = jnp.where(sel, -jnp.inf, cur)
        vals.append(m)
    return rank, vals


def _staircase_counts(v1, v2, tb):
    v2_all = jnp.concatenate(v2, axis=0)
    cands, poss, sizes = [], [], []
    for r1 in range(PEER_TOPK):
        rows = PEER_TOPK if r1 == 0 else 8
        r2 = lax.broadcasted_iota(jnp.int32, (rows, tb), 0)
        cands.append(jnp.where(r2 < PEER_TOPK // (r1 + 1), v1[r1] + v2_all[:rows], -jnp.inf))
        poss.append(r2 + r1 * PEER_TOPK)
        sizes.append(rows)
    cand = jnp.concatenate(cands, axis=0)
    pos = jnp.concatenate(poss, axis=0)
    cur = cand
    picked = jnp.zeros(cand.shape, F32)
    for _ in range(PEER_TOPK):
        m = jnp.max(cur, axis=0, keepdims=True)
        idx = jnp.min(jnp.where(cur == m, pos, PEER_TOPK * PEER_TOPK), axis=0, keepdims=True)
        sel = pos == idx
        picked = jnp.where(sel, 1.0, picked)
        cur = jnp.where(sel, -jnp.inf, cur)
    z = jnp.sum(picked * jnp.exp(cand - (v1[0] + v2[0])), axis=0, keepdims=True)
    counts, lo = [], 0
    for rows in sizes:
        counts.append(jnp.sum(picked[lo:lo + rows], axis=0, keepdims=True))
        lo += rows
    return counts, z


def _route_kernel(hnt_ref, wqt_ref, k1_ref, k2_ref, c1_ref, q1_ref, c2_ref, p2_ref, qt_s):
    tb = hnt_ref.shape[1]
    half = PEER_D_KEY // 2
    qt_s[...] = jnp.dot(wqt_ref[...], hnt_ref[...], preferred_element_type=F32)

    def head(h, carry):
        base = pl.multiple_of(h * PEER_D_KEY, PEER_D_KEY)
        qa = qt_s[pl.ds(base, half), :].astype(BF16)
        qb = qt_s[pl.ds(base + half, half), :].astype(BF16)
        s1 = jnp.dot(k1_ref[h], qa, preferred_element_type=F32)
        s2 = jnp.dot(k2_ref[h], qb, preferred_element_type=F32)
        rank1, v1 = _top16(s1)
        rank2, v2 = _top16(s2)
        counts, z = _staircase_counts(v1, v2, tb)
        c1 = jnp.zeros(s1.shape, F32)
        for r1 in range(PEER_TOPK):
            c1 = jnp.where(rank1 == r1, counts[r1], c1)
        c1_ref[h] = c1
        q1_ref[h] = jnp.exp(s1 - v1[0]) / z
        c2_ref[h] = rank2.astype(F32).astype(BF16)
        p2_ref[h] = jnp.exp(s2 - v2[0]).astype(BF16)
        return carry

    lax.fori_loop(0, PEER_HEADS, head, 0)


def _route(hnt, wqt, keys1, keys2, tb):
    d, t = hnt.shape
    nq = wqt.shape[0]
    full = lambda shape: pl.BlockSpec(shape, lambda i: (0,) * len(shape))
    code = pl.BlockSpec((PEER_HEADS, PEER_KEYS, tb), lambda i: (0, 0, i))
    code_shape = lambda dt: jax.ShapeDtypeStruct((PEER_HEADS, PEER_KEYS, t), dt)
    return pl.pallas_call(
        _route_kernel,
        grid=(t // tb,),
        in_specs=[pl.BlockSpec((d, tb), lambda i: (0, i)), full((nq, d)),
                  full(keys1.shape), full(keys2.shape)],
        out_specs=[code] * 4,
        out_shape=[code_shape(F32), code_shape(F32), code_shape(BF16), code_shape(BF16)],
        scratch_shapes=[pltpu.VMEM((nq, tb), F32)],
        compiler_params=pltpu.CompilerParams(dimension_semantics=("arbitrary",),
                                             vmem_limit_bytes=VMEM_LIMIT),
        name="route",
    )(hnt, wqt, keys1, keys2)


def _experts_kernel(hnt_ref, u_ref, vt_ref, c1_ref, q1_ref, c2_ref, p2_ref, x1_ref, o_ref,
                    acc_s, st_s, z_s, hnt_s, c2_s, p2_s, *, n_eb, n_blocks):
    t = pl.program_id(0)
    eb = u_ref.shape[0]
    tb = hnt_ref.shape[1]
    keys_per_block = eb // PEER_KEYS
    j_b = jnp.clip(t - 1, 0, n_blocks - 1) % n_eb
    j_c = jnp.clip(t - 2, 0, n_blocks - 1) % n_eb

    @pl.when(t == 0)
    def _():
        st_s[...] = jnp.zeros_like(st_s)
        z_s[...] = jnp.zeros_like(z_s)
        acc_s[...] = jnp.zeros_like(acc_s)

    @pl.when(jnp.logical_and(t >= 2, j_c == 0))
    def _():
        acc_s[...] = jnp.zeros_like(acc_s)

    @pl.when(jnp.minimum(t, n_blocks - 1) % n_eb == 0)
    def _():
        hnt_s[...] = hnt_ref[...]

    @pl.when(j_b == 0)
    def _():
        c2_s[...] = c2_ref[...]
        p2_s[...] = p2_ref[...]

    row0 = (j_b * keys_per_block) % CODE_ROW_BLOCK
    rows_of = lambda ref, ii: [ref[h, pl.ds(row0 + ii, 1), :] for h in range(PEER_HEADS)]
    c1_all = [rows_of(c1_ref, ii) for ii in range(keys_per_block)]
    q1_all = [rows_of(q1_ref, ii) for ii in range(keys_per_block)]
    packed = (PEER_KEYS // BF16_ROWS, BF16_ROWS, LANES)
    row_bf16 = lambda r: jnp.broadcast_to(r, (BF16_ROWS, LANES)).astype(BF16)[None]

    def stage_a(cols):
        st_s[:, cols] = jnp.dot(u_ref[...], hnt_s[:, cols], preferred_element_type=F32)

    def stage_b(ii, cols):
        rows = slice(ii * PEER_KEYS, (ii + 1) * PEER_KEYS)
        gate = None
        for h in range(PEER_HEADS):
            c1 = row_bf16(c1_all[ii][h][:, cols])
            q1 = row_bf16(q1_all[ii][h][:, cols])
            term = jnp.where(c2_s[h, :, cols].reshape(packed) < c1,
                             p2_s[h, :, cols].reshape(packed) * q1, jnp.zeros((), BF16))
            gate = term if gate is None else gate + term
        act = jax.nn.gelu(st_s[rows, cols]).astype(BF16).reshape(packed)
        z_s[rows, cols] = (act * gate).reshape(PEER_KEYS, LANES)

    def stage_c(cols):
        acc_s[:, cols] += jnp.dot(vt_ref[...], z_s[:, cols], preferred_element_type=F32)

    for mc in range(tb // MXU_COLS):
        stage_c(slice(mc * MXU_COLS, (mc + 1) * MXU_COLS))
        for lc in range(MXU_COLS // LANES):
            lo = mc * MXU_COLS + lc * LANES
            for ii in range(keys_per_block):
                stage_b(ii, slice(lo, lo + LANES))
        stage_a(slice(mc * MXU_COLS, (mc + 1) * MXU_COLS))

    @pl.when(jnp.logical_and(t >= 2, j_c == n_eb - 1))
    def _():
        o_ref[...] = x1_ref[...] + acc_s[...].T


def _experts(hnt, u_tab, vt_blocks, codes, x1, tb):
    d, t = hnt.shape
    n_eb, _, eb = vt_blocks.shape
    keys_per_block = eb // PEER_KEYS
    assert CODE_ROW_BLOCK % keys_per_block == 0
    n_blocks = (t // tb) * n_eb
    g_a = lambda s: jnp.minimum(s, n_blocks - 1)
    g_b = lambda s: jnp.clip(s - 1, 0, n_blocks - 1)
    g_c = lambda s: jnp.clip(s - 2, 0, n_blocks - 1)
    code_rows = pl.BlockSpec(
        (PEER_HEADS, CODE_ROW_BLOCK, tb),
        lambda s: (0, (g_b(s) % n_eb) * keys_per_block // CODE_ROW_BLOCK, g_b(s) // n_eb))
    code_keys = pl.BlockSpec((PEER_HEADS, PEER_KEYS, tb), lambda s: (0, 0, g_b(s) // n_eb))
    assert codes[0].dtype == F32 and codes[1].dtype == F32
    assert codes[2].dtype == BF16 and codes[3].dtype == BF16
    return pl.pallas_call(
        functools.partial(_experts_kernel, n_eb=n_eb, n_blocks=n_blocks),
        grid=(n_blocks + 2,),
        in_specs=[pl.BlockSpec((d, tb), lambda s: (0, g_a(s) // n_eb)),
                  pl.BlockSpec((eb, d), lambda s: (g_a(s) % n_eb, 0)),
                  pl.BlockSpec((None, d, eb), lambda s: (g_c(s) % n_eb, 0, 0)),
                  code_rows, code_rows, code_keys, code_keys,
                  pl.BlockSpec((tb, d), lambda s: (g_c(s) // n_eb, 0))],
        out_specs=pl.BlockSpec((tb, d), lambda s: (g_c(s) // n_eb, 0)),
        out_shape=jax.ShapeDtypeStruct((t, d), F32),
        scratch_shapes=[pltpu.VMEM((d, tb), F32), pltpu.VMEM((eb, tb), F32),
                        pltpu.VMEM((eb, tb), BF16), pltpu.VMEM((d, tb), BF16),
                        pltpu.VMEM((PEER_HEADS, PEER_KEYS, tb), BF16),
                        pltpu.VMEM((PEER_HEADS, PEER_KEYS, tb), BF16)],
        compiler_params=pltpu.CompilerParams(dimension_semantics=("arbitrary",),
                                             vmem_limit_bytes=VMEM_LIMIT),
        name="experts",
    )(hnt, u_tab, vt_blocks, *codes, x1)


def kernel(x, ln1_w, w_in, q_norm_w, k_norm_w, sgu_norm_w, sgu_w, sgu_b, mix_norm_w, w_out,
           ln2_w, peer_w_query, peer_keys1, peer_keys2, peer_u, peer_v, rel_bias):
    b, s, d = x.shape
    t = b * s
    bias = _bias_tables(rel_bias)
    for l in range(ln1_w.shape[0]):
        x2 = x.reshape(t, d)
        qkv, sgu_n = _in_proj(x2, ln1_w[l][None], w_in[l].astype(BF16), sgu_norm_w[l], sgu_w[l],
                              sgu_b[l], mix_norm_w[l, D_ATTN:][None], tm=512)
        attn = _attention(qkv.reshape(b, s, 3 * D_ATTN), q_norm_w[l][None], k_norm_w[l][None], bias)
        x1, hnt = _out_proj(attn.reshape(t, D_ATTN), sgu_n, x2, mix_norm_w[l, :D_ATTN][None],
                            w_out[l].astype(BF16), ln2_w[l][None], tm=512)
        codes = _route(hnt, peer_w_query[l].T.astype(BF16), peer_keys1[l].astype(BF16),
                       peer_keys2[l].astype(BF16), tb=256)
        vt_blocks = peer_v[l].reshape(-1, EXPERT_BLOCK, d).transpose(0, 2, 1).astype(BF16)
        out = _experts(hnt, peer_u[l].astype(BF16), vt_blocks, codes, x1, tb=EXPERT_TOKENS)
        x = out.reshape(b, s, d)
    return x
```

```python
import functools
import math

import numpy as np
import jax
import jax.numpy as jnp
from jax import lax
from jax.experimental import pallas as pl
from jax.experimental.pallas import tpu as pltpu

F32 = jnp.float32
BF16 = jnp.bfloat16

ATTN_HEADS = 8
HEAD_DIM = 64
D_ATTN = ATTN_HEADS * HEAD_DIM
SGU_GROUPS = 4
SGU_CH = 128
D_SGU = SGU_GROUPS * SGU_CH
CHUNK = 128
DILATED = ((128, 1), (512, 4), (2048, 16))
BLK = 128
NUM_BUCKETS = 32
MAX_DISTANCE = 2048
PEER_HEADS = 8
PEER_KEYS = 128
PEER_D_KEY = 256
PEER_TOPK = 16
EPS = 1e-6
NEG = -1e30

LANES = 128
BF16_ROWS = 16
MXU_COLS = 256
CODE_ROW_BLOCK = 8
ATTN_GROUP = 4
ROUTE_HEADS = 2
EXPERT_BLOCK = 1024
EXPERT_TOKENS = 512
VMEM_LIMIT = 56 * 1024 * 1024


def _rms_rows(x, w):
    return x * lax.rsqrt(jnp.mean(x * x, axis=-1, keepdims=True) + EPS) * w


def _bucket_tables():
    qi = np.arange(BLK)[:, None]
    kj = np.arange(2 * BLK)[None, :]
    rel = BLK + qi - kj
    max_exact = NUM_BUCKETS // 2
    buckets, valid = [], []
    for window, dil in DILATED:
        w_sub = window // dil
        dist = np.maximum(rel, 0) * dil
        d_f = np.maximum(dist, max_exact).astype(np.float32)
        large = max_exact + (np.log(d_f / np.float32(max_exact)) / np.float32(math.log(MAX_DISTANCE / max_exact))
                             * np.float32(NUM_BUCKETS - max_exact)).astype(np.int32)
        large = np.minimum(large, NUM_BUCKETS - 1)
        buckets.append(np.where(dist < max_exact, dist, large).astype(np.int32))
        valid.append(((rel >= 0) & (rel <= w_sub)).astype(np.int32))
    return np.stack(buckets), np.stack(valid)


def _bias_kernel(rb_ref, bkt_ref, valid_ref, o_ref):
    h = pl.program_id(1)
    bkt = bkt_ref[...]
    acc = jnp.zeros(bkt.shape, F32)
    for j in range(NUM_BUCKETS):
        acc = jnp.where(bkt == j, rb_ref[j, h], acc)
    o_ref[...] = jnp.where(valid_ref[...] > 0, acc, NEG)


def _bias_tables(rel_bias):
    bkt, valid = _bucket_tables()
    npat = len(DILATED)
    blk = pl.BlockSpec((None, BLK, 2 * BLK), lambda p, h: (p, 0, 0))
    return pl.pallas_call(
        _bias_kernel,
        grid=(npat, ATTN_HEADS),
        in_specs=[pl.BlockSpec(memory_space=pltpu.SMEM), blk, blk],
        out_specs=pl.BlockSpec((None, None, BLK, 2 * BLK), lambda p, h: (p, h, 0, 0)),
        out_shape=jax.ShapeDtypeStruct((npat, ATTN_HEADS, BLK, 2 * BLK), F32),
        name="bias",
    )(rel_bias, jnp.asarray(bkt), jnp.asarray(valid))


def _in_proj_kernel(x_ref, ln1_ref, win_ref, sgnw_ref, sguw_ref, bst_ref, mixw_ref,
                    qkv_ref, sgu_ref, sgu_s):
    tm = x_ref.shape[0]
    h = _rms_rows(x_ref[...], ln1_ref[...]).astype(BF16)
    proj = jnp.dot(h, win_ref[...], preferred_element_type=F32)
    qkv_ref[...] = proj[:, :3 * D_ATTN]
    row = lax.broadcasted_iota(jnp.int32, (CHUNK, CHUNK), 0)
    col = lax.broadcasted_iota(jnp.int32, (CHUNK, CHUNK), 1)
    for g in range(SGU_GROUPS):
        u_lo = 3 * D_ATTN + g * SGU_CH
        v_lo = 3 * D_ATTN + D_SGU + g * SGU_CH
        ua = jax.nn.gelu(proj[:, u_lo:u_lo + SGU_CH])
        va = jax.nn.gelu(proj[:, v_lo:v_lo + SGU_CH])
        vn = _rms_rows(va, sgnw_ref[g:g + 1, :]).astype(BF16)
        w = jnp.where(row >= col, sguw_ref[g], 0.0).astype(BF16)
        bias = bst_ref[:, g:g + 1]
        for c in range(tm // CHUNK):
            rows = slice(c * CHUNK, (c + 1) * CHUNK)
            spatial = jnp.dot(w, vn[rows], preferred_element_type=F32) + bias
            sgu_s[rows, g * SGU_CH:(g + 1) * SGU_CH] = ua[rows] * spatial
    sgu_ref[...] = _rms_rows(sgu_s[...], mixw_ref[...]).astype(BF16)


def _in_proj(x2, ln1, w_in, sgu_norm_w, sgu_w, sgu_b, mixw_sgu, tm):
    t, d = x2.shape
    dproj = w_in.shape[1]
    full = lambda shape: pl.BlockSpec(shape, lambda i: (0,) * len(shape))
    return pl.pallas_call(
        _in_proj_kernel,
        grid=(t // tm,),
        in_specs=[pl.BlockSpec((tm, d), lambda i: (i, 0)),
                  full((1, d)), full((d, dproj)), full((SGU_GROUPS, SGU_CH)),
                  full((SGU_GROUPS, CHUNK, CHUNK)), full((CHUNK, SGU_GROUPS)), full((1, D_SGU))],
        out_specs=[pl.BlockSpec((tm, 3 * D_ATTN), lambda i: (i, 0)),
                   pl.BlockSpec((tm, D_SGU), lambda i: (i, 0))],
        out_shape=[jax.ShapeDtypeStruct((t, 3 * D_ATTN), F32),
                   jax.ShapeDtypeStruct((t, D_SGU), BF16)],
        scratch_shapes=[pltpu.VMEM((tm, D_SGU), F32)],
        compiler_params=pltpu.CompilerParams(dimension_semantics=("arbitrary",),
                                             vmem_limit_bytes=VMEM_LIMIT),
        name="in_proj",
    )(x2, ln1, w_in, sgu_norm_w, sgu_w, sgu_b.T, mixw_sgu)


def _attn_kernel(q_ref, k_ref, v_ref, qw_ref, kw_ref, bias_ref, o_ref,
                 qs, ks, vs, acc_s, m_s):
    s_len = q_ref.shape[0]
    heads = LANES // HEAD_DIM
    lane = lax.broadcasted_iota(jnp.int32, (s_len, LANES), 1)
    r_i = lax.broadcasted_iota(jnp.int32, (LANES, LANES), 0) // HEAD_DIM
    c_i = lax.broadcasted_iota(jnp.int32, (LANES, LANES), 1) // HEAD_DIM
    averager = jnp.where(r_i == c_i, 1.0 / HEAD_DIM, 0.0).astype(BF16)

    def head_rms(x, w):
        sq = x * x
        hi = sq.astype(BF16)
        lo = (sq - hi.astype(F32)).astype(BF16)
        mean = (jnp.dot(hi, averager, preferred_element_type=F32)
                + jnp.dot(lo, averager, preferred_element_type=F32))
        return x * lax.rsqrt(mean + EPS) * w

    qn = head_rms(q_ref[...], qw_ref[...]) * (HEAD_DIM ** -0.5)
    ks[...] = head_rms(k_ref[...], kw_ref[...])
    v = v_ref[...]
    for hh in range(heads):
        mine = (lane // HEAD_DIM) == hh
        qs[hh] = jnp.where(mine, qn, 0.0)
        vs[hh] = jnp.where(mine, v, 1.0)

    for p, (_, dil) in enumerate(DILATED):
        nblk = s_len // dil // BLK
        shift = dil.bit_length() - 1

        def group(qstarts, first, p=p, dil=dil):
            nk = BLK if first else 2 * BLK
            chains = [(hh, q0) for q0 in qstarts for hh in range(heads)]
            rows = lambda q0, n: pl.ds(q0, n, stride=dil)
            logits = []
            for hh, q0 in chains:
                qb = qs[hh, rows(q0, BLK), :].astype(BF16)
                kb = ks[rows(q0 - (nk - BLK) * dil, nk), :].astype(BF16)
                s = lax.dot_general(qb, kb, (((1,), (1,)), ((), ())), preferred_element_type=F32)
                bias = bias_ref[p, hh, :, BLK:] if first else bias_ref[p, hh]
                logits.append(jnp.where(bias > 0.1 * NEG, s + bias, NEG))
            probs = []
            for (hh, q0), lg in zip(chains, logits):
                m = jnp.max(lg, axis=-1, keepdims=True)
                m_s[hh, p, rows(q0, BLK), :] = jnp.broadcast_to(m, (BLK, LANES))
                probs.append(jnp.exp(lg - m).astype(BF16))
            for (hh, q0), pe in zip(chains, probs):
                vb = vs[hh, rows(q0 - (nk - BLK) * dil, nk), :].astype(BF16)
                acc_s[hh, p, rows(q0, BLK), :] = jnp.dot(pe, vb, preferred_element_type=F32)

        def first_groups(g, carry, group=group):
            group([g * ATTN_GROUP + j for j in range(min(dil, ATTN_GROUP))], True)
            return carry

        def later_groups(g, carry, group=group, dil=dil, shift=shift):
            starts = []
            for j in range(ATTN_GROUP):
                i = dil + g * ATTN_GROUP + j
                starts.append(jnp.bitwise_and(i, dil - 1) + jnp.right_shift(i, shift) * (dil * BLK))
            group(starts, False)
            return carry

        lax.fori_loop(0, -(-dil // ATTN_GROUP), first_groups, 0)
        n_later = (nblk - 1) * dil
        lax.fori_loop(0, n_later // ATTN_GROUP, later_groups, 0)
        tail = [dil + n_later // ATTN_GROUP * ATTN_GROUP + j for j in range(n_later % ATTN_GROUP)]
        if tail:
            group([(i & (dil - 1)) + (i >> shift) * (dil * BLK) for i in tail], False)

    out = None
    for hh in range(heads):
        ms = [m_s[hh, p] for p in range(len(DILATED))]
        m_all = functools.reduce(jnp.maximum, ms)
        tot = sum(jnp.exp(m - m_all) * acc_s[hh, p] for p, m in enumerate(ms))
        ratio = tot / pltpu.roll(tot, HEAD_DIM, axis=1)
        out = ratio if out is None else jnp.where((lane // HEAD_DIM) == hh, ratio, out)
    o_ref[...] = out


def _attention(qkv3, q_norm_w, k_norm_w, bias):
    b, s, _ = qkv3.shape
    nhp = D_ATTN // LANES
    heads = LANES // HEAD_DIM
    npat = len(DILATED)
    col = lambda off: pl.BlockSpec((None, s, LANES), lambda bi, hp, off=off: (bi, 0, off + hp))
    return pl.pallas_call(
        _attn_kernel,
        grid=(b, nhp),
        in_specs=[col(0), col(nhp), col(2 * nhp),
                  pl.BlockSpec((1, LANES), lambda bi, hp: (0, 0)),
                  pl.BlockSpec((1, LANES), lambda bi, hp: (0, 0)),
                  pl.BlockSpec((npat, LANES // HEAD_DIM, BLK, 2 * BLK), lambda bi, hp: (0, hp, 0, 0))],
        out_specs=pl.BlockSpec((None, s, LANES), lambda bi, hp: (bi, 0, hp)),
        out_shape=jax.ShapeDtypeStruct((b, s, D_ATTN), F32),
        scratch_shapes=[pltpu.VMEM((heads, s, LANES), F32), pltpu.VMEM((s, LANES), F32),
                        pltpu.VMEM((heads, s, LANES), F32),
                        pltpu.VMEM((heads, npat, s, LANES), F32),
                        pltpu.VMEM((heads, npat, s, LANES), F32)],
        compiler_params=pltpu.CompilerParams(dimension_semantics=("arbitrary", "arbitrary"),
                                             vmem_limit_bytes=VMEM_LIMIT),
        name="attn",
    )(qkv3, qkv3, qkv3, jnp.tile(q_norm_w, (1, heads)), jnp.tile(k_norm_w, (1, heads)), bias)


def _out_proj_kernel(attn_ref, sgu_ref, x_ref, mixw_ref, wout_ref, ln2_ref, x1_ref, hnt_ref):
    an = _rms_rows(attn_ref[...], mixw_ref[...]).astype(BF16)
    y = jnp.dot(an, wout_ref[:D_ATTN, :], preferred_element_type=F32)
    y = y + jnp.dot(sgu_ref[...], wout_ref[D_ATTN:, :], preferred_element_type=F32)
    x1 = x_ref[...] + y
    x1_ref[...] = x1
    hnt_ref[...] = _rms_rows(x1, ln2_ref[...]).T.astype(BF16)


def _out_proj(attn2, sgu_n, x2, mixw_attn, w_out, ln2, tm):
    t, d = x2.shape
    full = lambda shape: pl.BlockSpec(shape, lambda i: (0,) * len(shape))
    return pl.pallas_call(
        _out_proj_kernel,
        grid=(t // tm,),
        in_specs=[pl.BlockSpec((tm, D_ATTN), lambda i: (i, 0)),
                  pl.BlockSpec((tm, D_SGU), lambda i: (i, 0)),
                  pl.BlockSpec((tm, d), lambda i: (i, 0)),
                  full((1, D_ATTN)), full((D_ATTN + D_SGU, d)), full((1, d))],
        out_specs=[pl.BlockSpec((tm, d), lambda i: (i, 0)),
                   pl.BlockSpec((d, tm), lambda i: (0, i))],
        out_shape=[jax.ShapeDtypeStruct((t, d), F32), jax.ShapeDtypeStruct((d, t), BF16)],
        compiler_params=pltpu.CompilerParams(dimension_semantics=("arbitrary",),
                                             vmem_limit_bytes=VMEM_LIMIT),
        name="out_proj",
    )(attn2, sgu_n, x2, mixw_attn, w_out, ln2)


def _extract16(x, order, tie_break):
    rank = jnp.full(x.shape, PEER_TOPK, jnp.int32)
    vals = []
    cur = x
    for r in range(PEER_TOPK):
        m = jnp.max(cur, axis=0, keepdims=True)
        sel = cur == m
        if tie_break:
            first = jnp.min(jnp.where(sel, order, jnp.iinfo(jnp.int32).max), axis=0, keepdims=True)
            sel = order == first
        rank = jnp.where(sel, r, rank)
        cur = jnp.where(sel, -jnp.inf, cur)
        vals.append(m)
    return rank, vals


def _taken(rank):
    return jnp.sum(jnp.where(rank < PEER_TOPK, 1.0, 0.0))


def _staircase_counts(v1, v2, tb, tie_break):
    v2_all = jnp.concatenate(v2, axis=0)
    cands, poss, sizes = [], [], []
    for r1 in range(PEER_TOPK):
        rows = PEER_TOPK if r1 == 0 else 8
        r2 = lax.broadcasted_iota(jnp.int32, (rows, tb), 0)
        cands.append(jnp.where(r2 < PEER_TOPK // (r1 + 1), v1[r1] + v2_all[:rows], -jnp.inf))
        poss.append(r2 + r1 * PEER_TOPK)
        sizes.append(rows)
    cand = jnp.concatenate(cands, axis=0)
    pos = jnp.concatenate(poss, axis=0)
    rank, _ = _extract16(cand, pos, tie_break)
    picked = jnp.where(rank < PEER_TOPK, 1.0, 0.0)
    z = jnp.sum(picked * jnp.exp(cand - (v1[0] + v2[0])), axis=0, keepdims=True)
    counts, lo = [], 0
    for rows in sizes:
        counts.append(jnp.sum(picked[lo:lo + rows], axis=0, keepdims=True))
        lo += rows
    return counts, z, _taken(rank)


def _head_codes(s1, s2, tie_break):
    tb = s1.shape[1]
    iota = lax.broadcasted_iota(jnp.int32, s1.shape, 0)
    rank1, v1 = _extract16(s1, iota, tie_break)
    rank2, v2 = _extract16(s2, iota, tie_break)
    counts, z, taken = _staircase_counts(v1, v2, tb, tie_break)
    c1 = jnp.zeros(s1.shape, F32)
    for r1 in range(PEER_TOPK):
        c1 = jnp.where(rank1 == r1, counts[r1], c1)
    q1 = jnp.exp(s1 - v1[0]) / z
    c2 = rank2.astype(F32).astype(BF16)
    p2 = jnp.exp(s2 - v2[0]).astype(BF16)
    return (c1, q1, c2, p2), taken + _taken(rank1) + _taken(rank2)


def _route_kernel(hnt_ref, wqt_ref, k1_ref, k2_ref, c1_ref, q1_ref, c2_ref, p2_ref, qt_s):
    tb = hnt_ref.shape[1]
    half = PEER_D_KEY // 2
    qt_s[...] = jnp.dot(wqt_ref[...], hnt_ref[...], preferred_element_type=F32)

    def scores(h):
        base = pl.multiple_of(h * PEER_D_KEY, PEER_D_KEY)
        qa = qt_s[pl.ds(base, half), :].astype(BF16)
        qb = qt_s[pl.ds(base + half, half), :].astype(BF16)
        return (jnp.dot(k1_ref[h], qa, preferred_element_type=F32),
                jnp.dot(k2_ref[h], qb, preferred_element_type=F32))

    def heads(g, carry):
        hs = [g * ROUTE_HEADS + j for j in range(ROUTE_HEADS)]
        ss = [scores(h) for h in hs]
        fast = [_head_codes(s1, s2, tie_break=False) for s1, s2 in ss]
        for h, (s1, s2), (codes, taken) in zip(hs, ss, fast):
            untied = taken == float(3 * PEER_TOPK * tb)
            codes = lax.cond(untied, lambda codes=codes: codes,
                             lambda s1=s1, s2=s2: _head_codes(s1, s2, tie_break=True)[0])
            for ref, code in zip((c1_ref, q1_ref, c2_ref, p2_ref), codes):
                ref[h] = code
        return carry

    lax.fori_loop(0, PEER_HEADS // ROUTE_HEADS, heads, 0)


def _route(hnt, wqt, keys1, keys2, tb):
    d, t = hnt.shape
    nq = wqt.shape[0]
    full = lambda shape: pl.BlockSpec(shape, lambda i: (0,) * len(shape))
    code = pl.BlockSpec((PEER_HEADS, PEER_KEYS, tb), lambda i: (0, 0, i))
    code_shape = lambda dt: jax.ShapeDtypeStruct((PEER_HEADS, PEER_KEYS, t), dt)
    return pl.pallas_call(
        _route_kernel,
        grid=(t // tb,),
        in_specs=[pl.BlockSpec((d, tb), lambda i: (0, i)), full((nq, d)),
                  full(keys1.shape), full(keys2.shape)],
        out_specs=[code] * 4,
        out_shape=[code_shape(F32), code_shape(F32), code_shape(BF16), code_shape(BF16)],
        scratch_shapes=[pltpu.VMEM((nq, tb), F32)],
        compiler_params=pltpu.CompilerParams(dimension_semantics=("arbitrary",),
                                             vmem_limit_bytes=VMEM_LIMIT),
        name="route",
    )(hnt, wqt, keys1, keys2)


def _experts_kernel(hnt_ref, u_ref, vt_ref, c1_ref, q1_ref, c2_ref, p2_ref, x1_ref, o_ref,
                    acc_s, st_s, z_s, hnt_s, c2_s, p2_s, *, n_eb, n_blocks):
    t = pl.program_id(0)
    eb = u_ref.shape[0]
    tb = hnt_ref.shape[1]
    keys_per_block = eb // PEER_KEYS
    j_b = jnp.clip(t - 1, 0, n_blocks - 1) % n_eb
    j_c = jnp.clip(t - 2, 0, n_blocks - 1) % n_eb

    @pl.when(t == 0)
    def _():
        st_s[...] = jnp.zeros_like(st_s)
        z_s[...] = jnp.zeros_like(z_s)
        acc_s[...] = jnp.zeros_like(acc_s)

    @pl.when(jnp.logical_and(t >= 2, j_c == 0))
    def _():
        acc_s[...] = jnp.zeros_like(acc_s)

    @pl.when(jnp.minimum(t, n_blocks - 1) % n_eb == 0)
    def _():
        hnt_s[...] = hnt_ref[...]

    @pl.when(j_b == 0)
    def _():
        c2_s[...] = c2_ref[...]
        p2_s[...] = p2_ref[...]

    row0 = (j_b * keys_per_block) % CODE_ROW_BLOCK
    rows_of = lambda ref, ii: [ref[h, pl.ds(row0 + ii, 1), :] for h in range(PEER_HEADS)]
    c1_all = [rows_of(c1_ref, ii) for ii in range(keys_per_block)]
    q1_all = [rows_of(q1_ref, ii) for ii in range(keys_per_block)]
    packed = (PEER_KEYS // BF16_ROWS, BF16_ROWS, LANES)
    row_bf16 = lambda r: jnp.broadcast_to(r, (BF16_ROWS, LANES)).astype(BF16)[None]

    def stage_a(cols):
        st_s[:, cols] = jnp.dot(u_ref[...], hnt_s[:, cols], preferred_element_type=F32)

    def stage_b(ii, cols):
        rows = slice(ii * PEER_KEYS, (ii + 1) * PEER_KEYS)
        gate = None
        for h in range(PEER_HEADS):
            c1 = row_bf16(c1_all[ii][h][:, cols])
            q1 = row_bf16(q1_all[ii][h][:, cols])
            term = jnp.where(c2_s[h, :, cols].reshape(packed) < c1,
                             p2_s[h, :, cols].reshape(packed) * q1, jnp.zeros((), BF16))
            gate = term if gate is None else gate + term
        act = jax.nn.gelu(st_s[rows, cols]).astype(BF16).reshape(packed)
        z_s[rows, cols] = (act * gate).reshape(PEER_KEYS, LANES)

    def stage_c(cols):
        acc_s[:, cols] += jnp.dot(vt_ref[...], z_s[:, cols], preferred_element_type=F32)

    for mc in range(tb // MXU_COLS):
        stage_c(slice(mc * MXU_COLS, (mc + 1) * MXU_COLS))
        for lc in range(MXU_COLS // LANES):
            lo = mc * MXU_COLS + lc * LANES
            for ii in range(keys_per_block):
                stage_b(ii, slice(lo, lo + LANES))
        stage_a(slice(mc * MXU_COLS, (mc + 1) * MXU_COLS))

    @pl.when(jnp.logical_and(t >= 2, j_c == n_eb - 1))
    def _():
        o_ref[...] = x1_ref[...] + acc_s[...].T


def _experts(hnt, u_tab, vt_blocks, codes, x1, tb):
    d, t = hnt.shape
    n_eb, _, eb = vt_blocks.shape
    keys_per_block = eb // PEER_KEYS
    assert CODE_ROW_BLOCK % keys_per_block == 0
    n_blocks = (t // tb) * n_eb
    g_a = lambda s: jnp.minimum(s, n_blocks - 1)
    g_b = lambda s: jnp.clip(s - 1, 0, n_blocks - 1)
    g_c = lambda s: jnp.clip(s - 2, 0, n_blocks - 1)
    code_rows = pl.BlockSpec(
        (PEER_HEADS, CODE_ROW_BLOCK, tb),
        lambda s: (0, (g_b(s) % n_eb) * keys_per_block // CODE_ROW_BLOCK, g_b(s) // n_eb))
    code_keys = pl.BlockSpec((PEER_HEADS, PEER_KEYS, tb), lambda s: (0, 0, g_b(s) // n_eb))
    assert codes[0].dtype == F32 and codes[1].dtype == F32
    assert codes[2].dtype == BF16 and codes[3].dtype == BF16
    return pl.pallas_call(
        functools.partial(_experts_kernel, n_eb=n_eb, n_blocks=n_blocks),
        grid=(n_blocks + 2,),
        in_specs=[pl.BlockSpec((d, tb), lambda s: (0, g_a(s) // n_eb)),
                  pl.BlockSpec((eb, d), lambda s: (g_a(s) % n_eb, 0)),
                  pl.BlockSpec((None, d, eb), lambda s: (g_c(s) % n_eb, 0, 0)),
                  code_rows, code_rows, code_keys, code_keys,
                  pl.BlockSpec((tb, d), lambda s: (g_c(s) // n_eb, 0))],
        out_specs=pl.BlockSpec((tb, d), lambda s: (g_c(s) // n_eb, 0)),
        out_shape=jax.ShapeDtypeStruct((t, d), F32),
        scratch_shapes=[pltpu.VMEM((d, tb), F32), pltpu.VMEM((eb, tb), F32),
                        pltpu.VMEM((eb, tb), BF16), pltpu.VMEM((d, tb), BF16),
                        pltpu.VMEM((PEER_HEADS, PEER_KEYS, tb), BF16),
                        pltpu.VMEM((PEER_HEADS, PEER_KEYS, tb), BF16)],
        compiler_params=pltpu.CompilerParams(dimension_semantics=("arbitrary",),
                                             vmem_limit_bytes=VMEM_LIMIT),
        name="experts",
    )(hnt, u_tab, vt_blocks, *codes, x1)


def kernel(x, ln1_w, w_in, q_norm_w, k_norm_w, sgu_norm_w, sgu_w, sgu_b, mix_norm_w, w_out,
           ln2_w, peer_w_query, peer_keys1, peer_keys2, peer_u, peer_v, rel_bias):
    b, s, d = x.shape
    t = b * s
    bias = _bias_tables(rel_bias)
    for l in range(ln1_w.shape[0]):
        x2 = x.reshape(t, d)
        qkv, sgu_n = _in_proj(x2, ln1_w[l][None], w_in[l].astype(BF16), sgu_norm_w[l], sgu_w[l],
                              sgu_b[l], mix_norm_w[l, D_ATTN:][None], tm=512)
        attn = _attention(qkv.reshape(b, s, 3 * D_ATTN), q_norm_w[l][None], k_norm_w[l][None], bias)
        x1, hnt = _out_proj(attn.reshape(t, D_ATTN), sgu_n, x2, mix_norm_w[l, :D_ATTN][None],
                            w_out[l].astype(BF16), ln2_w[l][None], tm=512)
        codes = _route(hnt, peer_w_query[l].T.astype(BF16), peer_keys1[l].astype(BF16),
                       peer_keys2[l].astype(BF16), tb=256)
        vt_blocks = peer_v[l].reshape(-1, EXPERT_BLOCK, d).transpose(0, 2, 1).astype(BF16)
        out = _experts(hnt, peer_u[l].astype(BF16), vt_blocks, codes, x1, tb=EXPERT_TOKENS)
        x = out.reshape(b, s, d)
    return x
```

```python
import functools
import math

import numpy as np
import jax
import jax.numpy as jnp
from jax import lax
from jax.experimental import pallas as pl
from jax.experimental.pallas import tpu as pltpu

F32 = jnp.float32
BF16 = jnp.bfloat16

ATTN_HEADS = 8
HEAD_DIM = 64
D_ATTN = ATTN_HEADS * HEAD_DIM
SGU_GROUPS = 4
SGU_CH = 128
D_SGU = SGU_GROUPS * SGU_CH
CHUNK = 128
DILATED = ((128, 1), (512, 4), (2048, 16))
BLK = 128
NUM_BUCKETS = 32
MAX_DISTANCE = 2048
PEER_HEADS = 8
PEER_KEYS = 128
PEER_D_KEY = 256
PEER_TOPK = 16
EPS = 1e-6
NEG = -1e30

LANES = 128
BF16_ROWS = 16
MXU_COLS = 256
CODE_ROW_BLOCK = 8
ATTN_GROUP = 4
ROUTE_HEADS = 2
EXPERT_BLOCK = 512
EXPERT_TOKENS = 1024
VMEM_LIMIT = 56 * 1024 * 1024


def _rms_rows(x, w):
    return x * lax.rsqrt(jnp.mean(x * x, axis=-1, keepdims=True) + EPS) * w


def _bucket_tables():
    qi = np.arange(BLK)[:, None]
    kj = np.arange(2 * BLK)[None, :]
    rel = BLK + qi - kj
    max_exact = NUM_BUCKETS // 2
    buckets, valid = [], []
    for window, dil in DILATED:
        w_sub = window // dil
        dist = np.maximum(rel, 0) * dil
        d_f = np.maximum(dist, max_exact).astype(np.float32)
        large = max_exact + (np.log(d_f / np.float32(max_exact)) / np.float32(math.log(MAX_DISTANCE / max_exact))
                             * np.float32(NUM_BUCKETS - max_exact)).astype(np.int32)
        large = np.minimum(large, NUM_BUCKETS - 1)
        buckets.append(np.where(dist < max_exact, dist, large).astype(np.int32))
        valid.append(((rel >= 0) & (rel <= w_sub)).astype(np.int32))
    return np.stack(buckets), np.stack(valid)


def _bias_kernel(rb_ref, bkt_ref, valid_ref, o_ref):
    h = pl.program_id(1)
    bkt = bkt_ref[...]
    acc = jnp.zeros(bkt.shape, F32)
    for j in range(NUM_BUCKETS):
        acc = jnp.where(bkt == j, rb_ref[j, h], acc)
    o_ref[...] = jnp.where(valid_ref[...] > 0, acc, NEG)


def _bias_tables(rel_bias):
    bkt, valid = _bucket_tables()
    npat = len(DILATED)
    blk = pl.BlockSpec((None, BLK, 2 * BLK), lambda p, h: (p, 0, 0))
    return pl.pallas_call(
        _bias_kernel,
        grid=(npat, ATTN_HEADS),
        in_specs=[pl.BlockSpec(memory_space=pltpu.SMEM), blk, blk],
        out_specs=pl.BlockSpec((None, None, BLK, 2 * BLK), lambda p, h: (p, h, 0, 0)),
        out_shape=jax.ShapeDtypeStruct((npat, ATTN_HEADS, BLK, 2 * BLK), F32),
        name="bias",
    )(rel_bias, jnp.asarray(bkt), jnp.asarray(valid))


def _in_proj_kernel(x_ref, ln1_ref, win_ref, sgnw_ref, sguw_ref, bst_ref, mixw_ref,
                    qkv_ref, sgu_ref, sgu_s):
    tm = x_ref.shape[0]
    h = _rms_rows(x_ref[...], ln1_ref[...]).astype(BF16)
    proj = jnp.dot(h, win_ref[...], preferred_element_type=F32)
    qkv_ref[...] = proj[:, :3 * D_ATTN]
    row = lax.broadcasted_iota(jnp.int32, (CHUNK, CHUNK), 0)
    col = lax.broadcasted_iota(jnp.int32, (CHUNK, CHUNK), 1)
    for g in range(SGU_GROUPS):
        u_lo = 3 * D_ATTN + g * SGU_CH
        v_lo = 3 * D_ATTN + D_SGU + g * SGU_CH
        ua = jax.nn.gelu(proj[:, u_lo:u_lo + SGU_CH])
        va = jax.nn.gelu(proj[:, v_lo:v_lo + SGU_CH])
        vn = _rms_rows(va, sgnw_ref[g:g + 1, :]).astype(BF16)
        w = jnp.where(row >= col, sguw_ref[g], 0.0).astype(BF16)
        bias = bst_ref[:, g:g + 1]
        for c in range(tm // CHUNK):
            rows = slice(c * CHUNK, (c + 1) * CHUNK)
            spatial = jnp.dot(w, vn[rows], preferred_element_type=F32) + bias
            sgu_s[rows, g * SGU_CH:(g + 1) * SGU_CH] = ua[rows] * spatial
    sgu_ref[...] = _rms_rows(sgu_s[...], mixw_ref[...]).astype(BF16)


def _in_proj(x2, ln1, w_in, sgu_norm_w, sgu_w, sgu_b, mixw_sgu, tm):
    t, d = x2.shape
    dproj = w_in.shape[1]
    full = lambda shape: pl.BlockSpec(shape, lambda i: (0,) * len(shape))
    return pl.pallas_call(
        _in_proj_kernel,
        grid=(t // tm,),
        in_specs=[pl.BlockSpec((tm, d), lambda i: (i, 0)),
                  full((1, d)), full((d, dproj)), full((SGU_GROUPS, SGU_CH)),
                  full((SGU_GROUPS, CHUNK, CHUNK)), full((CHUNK, SGU_GROUPS)), full((1, D_SGU))],
        out_specs=[pl.BlockSpec((tm, 3 * D_ATTN), lambda i: (i, 0)),
                   pl.BlockSpec((tm, D_SGU), lambda i: (i, 0))],
        out_shape=[jax.ShapeDtypeStruct((t, 3 * D_ATTN), F32),
                   jax.ShapeDtypeStruct((t, D_SGU), BF16)],
        scratch_shapes=[pltpu.VMEM((tm, D_SGU), F32)],
        compiler_params=pltpu.CompilerParams(dimension_semantics=("arbitrary",),
                                             vmem_limit_bytes=VMEM_LIMIT),
        name="in_proj",
    )(x2, ln1, w_in, sgu_norm_w, sgu_w, sgu_b.T, mixw_sgu)


def _attn_kernel(q_ref, k_ref, v_ref, qw_ref, kw_ref, bias_ref, o_ref,
                 qs, ks, vs, acc_s, m_s):
    s_len = q_ref.shape[0]
    heads = LANES // HEAD_DIM
    lane = lax.broadcasted_iota(jnp.int32, (s_len, LANES), 1)
    r_i = lax.broadcasted_iota(jnp.int32, (LANES, LANES), 0) // HEAD_DIM
    c_i = lax.broadcasted_iota(jnp.int32, (LANES, LANES), 1) // HEAD_DIM
    averager = jnp.where(r_i == c_i, 1.0 / HEAD_DIM, 0.0).astype(BF16)

    def head_rms(x, w):
        sq = x * x
        hi = sq.astype(BF16)
        lo = (sq - hi.astype(F32)).astype(BF16)
        mean = (jnp.dot(hi, averager, preferred_element_type=F32)
                + jnp.dot(lo, averager, preferred_element_type=F32))
        return x * lax.rsqrt(mean + EPS) * w

    qn = head_rms(q_ref[...], qw_ref[...]) * (HEAD_DIM ** -0.5)
    ks[...] = head_rms(k_ref[...], kw_ref[...])
    v = v_ref[...]
    for hh in range(heads):
        mine = (lane // HEAD_DIM) == hh
        qs[hh] = jnp.where(mine, qn, 0.0)
        vs[hh] = jnp.where(mine, v, 1.0)

    for p, (_, dil) in enumerate(DILATED):
        nblk = s_len // dil // BLK
        shift = dil.bit_length() - 1

        def group(qstarts, first, p=p, dil=dil):
            nk = BLK if first else 2 * BLK
            chains = [(hh, q0) for q0 in qstarts for hh in range(heads)]
            rows = lambda q0, n: pl.ds(q0, n, stride=dil)
            logits = []
            for hh, q0 in chains:
                qb = qs[hh, rows(q0, BLK), :].astype(BF16)
                kb = ks[rows(q0 - (nk - BLK) * dil, nk), :].astype(BF16)
                s = lax.dot_general(qb, kb, (((1,), (1,)), ((), ())), preferred_element_type=F32)
                bias = bias_ref[p, hh, :, BLK:] if first else bias_ref[p, hh]
                logits.append(jnp.where(bias > 0.1 * NEG, s + bias, NEG))
            probs = []
            for (hh, q0), lg in zip(chains, logits):
                m = jnp.max(lg, axis=-1, keepdims=True)
                m_s[hh, p, rows(q0, BLK), :] = jnp.broadcast_to(m, (BLK, LANES))
                probs.append(jnp.exp(lg - m).astype(BF16))
            for (hh, q0), pe in zip(chains, probs):
                vb = vs[hh, rows(q0 - (nk - BLK) * dil, nk), :].astype(BF16)
                acc_s[hh, p, rows(q0, BLK), :] = jnp.dot(pe, vb, preferred_element_type=F32)

        def first_groups(g, carry, group=group):
            group([g * ATTN_GROUP + j for j in range(min(dil, ATTN_GROUP))], True)
            return carry

        def later_groups(g, carry, group=group, dil=dil, shift=shift):
            starts = []
            for j in range(ATTN_GROUP):
                i = dil + g * ATTN_GROUP + j
                starts.append(jnp.bitwise_and(i, dil - 1) + jnp.right_shift(i, shift) * (dil * BLK))
            group(starts, False)
            return carry

        lax.fori_loop(0, -(-dil // ATTN_GROUP), first_groups, 0)
        n_later = (nblk - 1) * dil
        lax.fori_loop(0, n_later // ATTN_GROUP, later_groups, 0)
        tail = [dil + n_later // ATTN_GROUP * ATTN_GROUP + j for j in range(n_later % ATTN_GROUP)]
        if tail:
            group([(i & (dil - 1)) + (i >> shift) * (dil * BLK) for i in tail], False)

    out = None
    for hh in range(heads):
        ms = [m_s[hh, p] for p in range(len(DILATED))]
        m_all = functools.reduce(jnp.maximum, ms)
        tot = sum(jnp.exp(m - m_all) * acc_s[hh, p] for p, m in enumerate(ms))
        ratio = tot / pltpu.roll(tot, HEAD_DIM, axis=1)
        out = ratio if out is None else jnp.where((lane // HEAD_DIM) == hh, ratio, out)
    o_ref[...] = out


def _attention(qkv3, q_norm_w, k_norm_w, bias):
    b, s, _ = qkv3.shape
    nhp = D_ATTN // LANES
    heads = LANES // HEAD_DIM
    npat = len(DILATED)
    col = lambda off: pl.BlockSpec((None, s, LANES), lambda bi, hp, off=off: (bi, 0, off + hp))
    return pl.pallas_call(
        _attn_kernel,
        grid=(b, nhp),
        in_specs=[col(0), col(nhp), col(2 * nhp),
                  pl.BlockSpec((1, LANES), lambda bi, hp: (0, 0)),
                  pl.BlockSpec((1, LANES), lambda bi, hp: (0, 0)),
                  pl.BlockSpec((npat, LANES // HEAD_DIM, BLK, 2 * BLK), lambda bi, hp: (0, hp, 0, 0))],
        out_specs=pl.BlockSpec((None, s, LANES), lambda bi, hp: (bi, 0, hp)),
        out_shape=jax.ShapeDtypeStruct((b, s, D_ATTN), F32),
        scratch_shapes=[pltpu.VMEM((heads, s, LANES), F32), pltpu.VMEM((s, LANES), F32),
                        pltpu.VMEM((heads, s, LANES), F32),
                        pltpu.VMEM((heads, npat, s, LANES), F32),
                        pltpu.VMEM((heads, npat, s, LANES), F32)],
        compiler_params=pltpu.CompilerParams(dimension_semantics=("arbitrary", "arbitrary"),
                                             vmem_limit_bytes=VMEM_LIMIT),
        name="attn",
    )(qkv3, qkv3, qkv3, jnp.tile(q_norm_w, (1, heads)), jnp.tile(k_norm_w, (1, heads)), bias)


def _out_proj_kernel(attn_ref, sgu_ref, x_ref, mixw_ref, wout_ref, ln2_ref, x1_ref, hnt_ref):
    an = _rms_rows(attn_ref[...], mixw_ref[...]).astype(BF16)
    y = jnp.dot(an, wout_ref[:D_ATTN, :], preferred_element_type=F32)
    y = y + jnp.dot(sgu_ref[...], wout_ref[D_ATTN:, :], preferred_element_type=F32)
    x1 = x_ref[...] + y
    x1_ref[...] = x1
    hnt_ref[...] = _rms_rows(x1, ln2_ref[...]).T.astype(BF16)


def _out_proj(attn2, sgu_n, x2, mixw_attn, w_out, ln2, tm):
    t, d = x2.shape
    full = lambda shape: pl.BlockSpec(shape, lambda i: (0,) * len(shape))
    return pl.pallas_call(
        _out_proj_kernel,
        grid=(t // tm,),
        in_specs=[pl.BlockSpec((tm, D_ATTN), lambda i: (i, 0)),
                  pl.BlockSpec((tm, D_SGU), lambda i: (i, 0)),
                  pl.BlockSpec((tm, d), lambda i: (i, 0)),
                  full((1, D_ATTN)), full((D_ATTN + D_SGU, d)), full((1, d))],
        out_specs=[pl.BlockSpec((tm, d), lambda i: (i, 0)),
                   pl.BlockSpec((d, tm), lambda i: (0, i))],
        out_shape=[jax.ShapeDtypeStruct((t, d), F32), jax.ShapeDtypeStruct((d, t), BF16)],
        compiler_params=pltpu.CompilerParams(dimension_semantics=("arbitrary",),
                                             vmem_limit_bytes=VMEM_LIMIT),
        name="out_proj",
    )(attn2, sgu_n, x2, mixw_attn, w_out, ln2)


def _extract16(x, order, tie_break):
    rank = jnp.full(x.shape, PEER_TOPK, jnp.int32)
    vals = []
    cur = x
    for r in range(PEER_TOPK):
        m = jnp.max(cur, axis=0, keepdims=True)
        sel = cur == m
        if tie_break:
            first = jnp.min(jnp.where(sel, order, jnp.iinfo(jnp.int32).max), axis=0, keepdims=True)
            sel = order == first
        rank = jnp.where(sel, r, rank)
        cur = jnp.where(sel, -jnp.inf, cur)
        vals.append(m)
    return rank, vals


def _taken(rank):
    return jnp.sum(jnp.where(rank < PEER_TOPK, 1.0, 0.0))


def _staircase_counts(v1, v2, tb, tie_break):
    v2_all = jnp.concatenate(v2, axis=0)
    cands, poss, sizes = [], [], []
    for r1 in range(PEER_TOPK):
        rows = PEER_TOPK if r1 == 0 else 8
        r2 = lax.broadcasted_iota(jnp.int32, (rows, tb), 0)
        cands.append(jnp.where(r2 < PEER_TOPK // (r1 + 1), v1[r1] + v2_all[:rows], -jnp.inf))
        poss.append(r2 + r1 * PEER_TOPK)
        sizes.append(rows)
    cand = jnp.concatenate(cands, axis=0)
    pos = jnp.concatenate(poss, axis=0)
    rank, _ = _extract16(cand, pos, tie_break)
    picked = jnp.where(rank < PEER_TOPK, 1.0, 0.0)
    z = jnp.sum(picked * jnp.exp(cand - (v1[0] + v2[0])), axis=0, keepdims=True)
    counts, lo = [], 0
    for rows in sizes:
        counts.append(jnp.sum(picked[lo:lo + rows], axis=0, keepdims=True))
        lo += rows
    return counts, z, _taken(rank)


def _head_codes(s1, s2, tie_break):
    tb = s1.shape[1]
    iota = lax.broadcasted_iota(jnp.int32, s1.shape, 0)
    rank1, v1 = _extract16(s1, iota, tie_break)
    rank2, v2 = _extract16(s2, iota, tie_break)
    counts, z, taken = _staircase_counts(v1, v2, tb, tie_break)
    c1 = jnp.zeros(s1.shape, F32)
    for r1 in range(PEER_TOPK):
        c1 = jnp.where(rank1 == r1, counts[r1], c1)
    q1 = jnp.exp(s1 - v1[0]) / z
    c2 = rank2.astype(F32).astype(BF16)
    p2 = jnp.exp(s2 - v2[0]).astype(BF16)
    return (c1, q1, c2, p2), taken + _taken(rank1) + _taken(rank2)


def _route_kernel(hnt_ref, wqt_ref, k1_ref, k2_ref, c1_ref, q1_ref, c2_ref, p2_ref, qt_s):
    tb = hnt_ref.shape[1]
    half = PEER_D_KEY // 2
    qt_s[...] = jnp.dot(wqt_ref[...], hnt_ref[...], preferred_element_type=F32)

    def scores(h):
        base = pl.multiple_of(h * PEER_D_KEY, PEER_D_KEY)
        qa = qt_s[pl.ds(base, half), :].astype(BF16)
        qb = qt_s[pl.ds(base + half, half), :].astype(BF16)
        return (jnp.dot(k1_ref[h], qa, preferred_element_type=F32),
                jnp.dot(k2_ref[h], qb, preferred_element_type=F32))

    def heads(g, carry):
        hs = [g * ROUTE_HEADS + j for j in range(ROUTE_HEADS)]
        ss = [scores(h) for h in hs]
        fast = [_head_codes(s1, s2, tie_break=False) for s1, s2 in ss]
        for h, (s1, s2), (codes, taken) in zip(hs, ss, fast):
            untied = taken == float(3 * PEER_TOPK * tb)
            codes = lax.cond(untied, lambda codes=codes: codes,
                             lambda s1=s1, s2=s2: _head_codes(s1, s2, tie_break=True)[0])
            for ref, code in zip((c1_ref, q1_ref, c2_ref, p2_ref), codes):
                ref[h] = code
        return carry

    lax.fori_loop(0, PEER_HEADS // ROUTE_HEADS, heads, 0)


def _route(hnt, wqt, keys1, keys2, tb):
    d, t = hnt.shape
    nq = wqt.shape[0]
    full = lambda shape: pl.BlockSpec(shape, lambda i: (0,) * len(shape))
    code = pl.BlockSpec((PEER_HEADS, PEER_KEYS, tb), lambda i: (0, 0, i))
    code_shape = lambda dt: jax.ShapeDtypeStruct((PEER_HEADS, PEER_KEYS, t), dt)
    return pl.pallas_call(
        _route_kernel,
        grid=(t // tb,),
        in_specs=[pl.BlockSpec((d, tb), lambda i: (0, i)), full((nq, d)),
                  full(keys1.shape), full(keys2.shape)],
        out_specs=[code] * 4,
        out_shape=[code_shape(F32), code_shape(F32), code_shape(BF16), code_shape(BF16)],
        scratch_shapes=[pltpu.VMEM((nq, tb), F32)],
        compiler_params=pltpu.CompilerParams(dimension_semantics=("arbitrary",),
                                             vmem_limit_bytes=VMEM_LIMIT),
        name="route",
    )(hnt, wqt, keys1, keys2)


def _experts_kernel(hnt_ref, u_ref, vt_ref, c1_ref, q1_ref, c2_ref, p2_ref, x1_ref, o_ref,
                    acc_s, st_s, z_s, hnt_s, c2_s, p2_s, *, n_eb, n_blocks):
    t = pl.program_id(0)
    eb = u_ref.shape[0]
    tb = hnt_ref.shape[1]
    keys_per_block = eb // PEER_KEYS
    j_b = jnp.clip(t - 1, 0, n_blocks - 1) % n_eb
    j_c = jnp.clip(t - 2, 0, n_blocks - 1) % n_eb

    @pl.when(t == 0)
    def _():
        st_s[...] = jnp.zeros_like(st_s)
        z_s[...] = jnp.zeros_like(z_s)
        acc_s[...] = jnp.zeros_like(acc_s)

    @pl.when(jnp.logical_and(t >= 2, j_c == 0))
    def _():
        acc_s[...] = jnp.zeros_like(acc_s)

    @pl.when(jnp.minimum(t, n_blocks - 1) % n_eb == 0)
    def _():
        hnt_s[...] = hnt_ref[...]

    @pl.when(j_b == 0)
    def _():
        c2_s[...] = c2_ref[...]
        p2_s[...] = p2_ref[...]

    row0 = (j_b * keys_per_block) % CODE_ROW_BLOCK
    rows_of = lambda ref, ii: [ref[h, pl.ds(row0 + ii, 1), :] for h in range(PEER_HEADS)]
    c1_all = [rows_of(c1_ref, ii) for ii in range(keys_per_block)]
    q1_all = [rows_of(q1_ref, ii) for ii in range(keys_per_block)]
    packed = (PEER_KEYS // BF16_ROWS, BF16_ROWS, LANES)
    row_bf16 = lambda r: jnp.broadcast_to(r, (BF16_ROWS, LANES)).astype(BF16)[None]

    def stage_a(cols):
        st_s[:, cols] = jnp.dot(u_ref[...], hnt_s[:, cols], preferred_element_type=F32)

    def stage_b(ii, cols):
        rows = slice(ii * PEER_KEYS, (ii + 1) * PEER_KEYS)
        gate = None
        for h in range(PEER_HEADS):
            c1 = row_bf16(c1_all[ii][h][:, cols])
            q1 = row_bf16(q1_all[ii][h][:, cols])
            term = jnp.where(c2_s[h, :, cols].reshape(packed) < c1,
                             p2_s[h, :, cols].reshape(packed) * q1, jnp.zeros((), BF16))
            gate = term if gate is None else gate + term
        act = jax.nn.gelu(st_s[rows, cols]).astype(BF16).reshape(packed)
        z_s[rows, cols] = (act * gate).reshape(PEER_KEYS, LANES)

    def stage_c(cols):
        acc_s[:, cols] += jnp.dot(vt_ref[...], z_s[:, cols], preferred_element_type=F32)

    for mc in range(tb // MXU_COLS):
        stage_c(slice(mc * MXU_COLS, (mc + 1) * MXU_COLS))
        for lc in range(MXU_COLS // LANES):
            lo = mc * MXU_COLS + lc * LANES
            for ii in range(keys_per_block):
                stage_b(ii, slice(lo, lo + LANES))
        stage_a(slice(mc * MXU_COLS, (mc + 1) * MXU_COLS))

    @pl.when(jnp.logical_and(t >= 2, j_c == n_eb - 1))
    def _():
        o_ref[...] = x1_ref[...] + acc_s[...].T


def _experts(hnt, u_tab, vt_blocks, codes, x1, tb):
    d, t = hnt.shape
    n_eb, _, eb = vt_blocks.shape
    keys_per_block = eb // PEER_KEYS
    assert CODE_ROW_BLOCK % keys_per_block == 0
    n_blocks = (t // tb) * n_eb
    g_a = lambda s: jnp.minimum(s, n_blocks - 1)
    g_b = lambda s: jnp.clip(s - 1, 0, n_blocks - 1)
    g_c = lambda s: jnp.clip(s - 2, 0, n_blocks - 1)
    code_rows = pl.BlockSpec(
        (PEER_HEADS, CODE_ROW_BLOCK, tb),
        lambda s: (0, (g_b(s) % n_eb) * keys_per_block // CODE_ROW_BLOCK, g_b(s) // n_eb))
    code_keys = pl.BlockSpec((PEER_HEADS, PEER_KEYS, tb), lambda s: (0, 0, g_b(s) // n_eb))
    assert codes[0].dtype == F32 and codes[1].dtype == F32
    assert codes[2].dtype == BF16 and codes[3].dtype == BF16
    return pl.pallas_call(
        functools.partial(_experts_kernel, n_eb=n_eb, n_blocks=n_blocks),
        grid=(n_blocks + 2,),
        in_specs=[pl.BlockSpec((d, tb), lambda s: (0, g_a(s) // n_eb)),
                  pl.BlockSpec((eb, d), lambda s: (g_a(s) % n_eb, 0)),
                  pl.BlockSpec((None, d, eb), lambda s: (g_c(s) % n_eb, 0, 0)),
                  code_rows, code_rows, code_keys, code_keys,
                  pl.BlockSpec((tb, d), lambda s: (g_c(s) // n_eb, 0))],
        out_specs=pl.BlockSpec((tb, d), lambda s: (g_c(s) // n_eb, 0)),
        out_shape=jax.ShapeDtypeStruct((t, d), F32),
        scratch_shapes=[pltpu.VMEM((d, tb), F32), pltpu.VMEM((eb, tb), F32),
                        pltpu.VMEM((eb, tb), BF16), pltpu.VMEM((d, tb), BF16),
                        pltpu.VMEM((PEER_HEADS, PEER_KEYS, tb), BF16),
                        pltpu.VMEM((PEER_HEADS, PEER_KEYS, tb), BF16)],
        compiler_params=pltpu.CompilerParams(dimension_semantics=("arbitrary",),
                                             vmem_limit_bytes=VMEM_LIMIT),
        name="experts",
    )(hnt, u_tab, vt_blocks, *codes, x1)


def kernel(x, ln1_w, w_in, q_norm_w, k_norm_w, sgu_norm_w, sgu_w, sgu_b, mix_norm_w, w_out,
           ln2_w, peer_w_query, peer_keys1, peer_keys2, peer_u, peer_v, rel_bias):
    b, s, d = x.shape
    t = b * s
    bias = _bias_tables(rel_bias)
    for l in range(ln1_w.shape[0]):
        x2 = x.reshape(t, d)
        qkv, sgu_n = _in_proj(x2, ln1_w[l][None], w_in[l].astype(BF16), sgu_norm_w[l], sgu_w[l],
                              sgu_b[l], mix_norm_w[l, D_ATTN:][None], tm=512)
        attn = _attention(qkv.reshape(b, s, 3 * D_ATTN), q_norm_w[l][None], k_norm_w[l][None], bias)
        x1, hnt = _out_proj(attn.reshape(t, D_ATTN), sgu_n, x2, mix_norm_w[l, :D_ATTN][None],
                            w_out[l].astype(BF16), ln2_w[l][None], tm=512)
        codes = _route(hnt, peer_w_query[l].T.astype(BF16), peer_keys1[l].astype(BF16),
                       peer_keys2[l].astype(BF16), tb=256)
        vt_blocks = peer_v[l].reshape(-1, EXPERT_BLOCK, d).transpose(0, 2, 1).astype(BF16)
        out = _experts(hnt, peer_u[l].astype(BF16), vt_blocks, codes, x1, tb=EXPERT_TOKENS)
        x = out.reshape(b, s, d)
    return x
```

```python
import functools
import math

import numpy as np
import jax
import jax.numpy as jnp
from jax import lax
from jax.experimental import pallas as pl
from jax.experimental.pallas import tpu as pltpu

F32 = jnp.float32
BF16 = jnp.bfloat16

ATTN_HEADS = 8
HEAD_DIM = 64
D_ATTN = ATTN_HEADS * HEAD_DIM
SGU_GROUPS = 4
SGU_CH = 128
D_SGU = SGU_GROUPS * SGU_CH
CHUNK = 128
DILATED = ((128, 1), (512, 4), (2048, 16))
BLK = 128
NUM_BUCKETS = 32
MAX_DISTANCE = 2048
PEER_HEADS = 8
PEER_KEYS = 128
PEER_D_KEY = 256
PEER_TOPK = 16
EPS = 1e-6
NEG = -1e30

LANES = 128
BF16_ROWS = 16
MXU_COLS = 256
CODE_ROW_BLOCK = 8
ATTN_GROUP = 4
ROUTE_HEADS = 2
MXU_GROUP_COLS = 2 * MXU_COLS
EXPERT_BLOCK = 1024
EXPERT_TOKENS = 512
VMEM_LIMIT = 56 * 1024 * 1024


def _rms_rows(x, w):
    return x * lax.rsqrt(jnp.mean(x * x, axis=-1, keepdims=True) + EPS) * w


def _bucket_tables():
    qi = np.arange(BLK)[:, None]
    kj = np.arange(2 * BLK)[None, :]
    rel = BLK + qi - kj
    max_exact = NUM_BUCKETS // 2
    buckets, valid = [], []
    for window, dil in DILATED:
        w_sub = window // dil
        dist = np.maximum(rel, 0) * dil
        d_f = np.maximum(dist, max_exact).astype(np.float32)
        large = max_exact + (np.log(d_f / np.float32(max_exact)) / np.float32(math.log(MAX_DISTANCE / max_exact))
                             * np.float32(NUM_BUCKETS - max_exact)).astype(np.int32)
        large = np.minimum(large, NUM_BUCKETS - 1)
        buckets.append(np.where(dist < max_exact, dist, large).astype(np.int32))
        valid.append(((rel >= 0) & (rel <= w_sub)).astype(np.int32))
    return np.stack(buckets), np.stack(valid)


def _bias_kernel(rb_ref, bkt_ref, valid_ref, o_ref):
    h = pl.program_id(1)
    bkt = bkt_ref[...]
    acc = jnp.zeros(bkt.shape, F32)
    for j in range(NUM_BUCKETS):
        acc = jnp.where(bkt == j, rb_ref[j, h], acc)
    o_ref[...] = jnp.where(valid_ref[...] > 0, acc, NEG)


def _bias_tables(rel_bias):
    bkt, valid = _bucket_tables()
    npat = len(DILATED)
    blk = pl.BlockSpec((None, BLK, 2 * BLK), lambda p, h: (p, 0, 0))
    return pl.pallas_call(
        _bias_kernel,
        grid=(npat, ATTN_HEADS),
        in_specs=[pl.BlockSpec(memory_space=pltpu.SMEM), blk, blk],
        out_specs=pl.BlockSpec((None, None, BLK, 2 * BLK), lambda p, h: (p, h, 0, 0)),
        out_shape=jax.ShapeDtypeStruct((npat, ATTN_HEADS, BLK, 2 * BLK), F32),
        name="bias",
    )(rel_bias, jnp.asarray(bkt), jnp.asarray(valid))


def _in_proj_kernel(x_ref, ln1_ref, win_ref, sgnw_ref, sguw_ref, bst_ref, mixw_ref,
                    qkv_ref, sgu_ref, sgu_s):
    tm = x_ref.shape[0]
    h = _rms_rows(x_ref[...], ln1_ref[...]).astype(BF16)
    proj = jnp.dot(h, win_ref[...], preferred_element_type=F32)
    qkv_ref[...] = proj[:, :3 * D_ATTN]
    row = lax.broadcasted_iota(jnp.int32, (CHUNK, CHUNK), 0)
    col = lax.broadcasted_iota(jnp.int32, (CHUNK, CHUNK), 1)
    for g in range(SGU_GROUPS):
        u_lo = 3 * D_ATTN + g * SGU_CH
        v_lo = 3 * D_ATTN + D_SGU + g * SGU_CH
        ua = jax.nn.gelu(proj[:, u_lo:u_lo + SGU_CH])
        va = jax.nn.gelu(proj[:, v_lo:v_lo + SGU_CH])
        vn = _rms_rows(va, sgnw_ref[g:g + 1, :]).astype(BF16)
        w = jnp.where(row >= col, sguw_ref[g], 0.0).astype(BF16)
        bias = bst_ref[:, g:g + 1]
        for c in range(tm // CHUNK):
            rows = slice(c * CHUNK, (c + 1) * CHUNK)
            spatial = jnp.dot(w, vn[rows], preferred_element_type=F32) + bias
            sgu_s[rows, g * SGU_CH:(g + 1) * SGU_CH] = ua[rows] * spatial
    sgu_ref[...] = _rms_rows(sgu_s[...], mixw_ref[...]).astype(BF16)


def _in_proj(x2, ln1, w_in, sgu_norm_w, sgu_w, sgu_b, mixw_sgu, tm):
    t, d = x2.shape
    dproj = w_in.shape[1]
    full = lambda shape: pl.BlockSpec(shape, lambda i: (0,) * len(shape))
    return pl.pallas_call(
        _in_proj_kernel,
        grid=(t // tm,),
        in_specs=[pl.BlockSpec((tm, d), lambda i: (i, 0)),
                  full((1, d)), full((d, dproj)), full((SGU_GROUPS, SGU_CH)),
                  full((SGU_GROUPS, CHUNK, CHUNK)), full((CHUNK, SGU_GROUPS)), full((1, D_SGU))],
        out_specs=[pl.BlockSpec((tm, 3 * D_ATTN), lambda i: (i, 0)),
                   pl.BlockSpec((tm, D_SGU), lambda i: (i, 0))],
        out_shape=[jax.ShapeDtypeStruct((t, 3 * D_ATTN), F32),
                   jax.ShapeDtypeStruct((t, D_SGU), BF16)],
        scratch_shapes=[pltpu.VMEM((tm, D_SGU), F32)],
        compiler_params=pltpu.CompilerParams(dimension_semantics=("arbitrary",),
                                             vmem_limit_bytes=VMEM_LIMIT),
        name="in_proj",
    )(x2, ln1, w_in, sgu_norm_w, sgu_w, sgu_b.T, mixw_sgu)


def _attn_kernel(q_ref, k_ref, v_ref, qw_ref, kw_ref, bias_ref, o_ref,
                 qs, ks, vs, acc_s, m_s):
    s_len = q_ref.shape[0]
    heads = LANES // HEAD_DIM
    lane = lax.broadcasted_iota(jnp.int32, (s_len, LANES), 1)
    r_i = lax.broadcasted_iota(jnp.int32, (LANES, LANES), 0) // HEAD_DIM
    c_i = lax.broadcasted_iota(jnp.int32, (LANES, LANES), 1) // HEAD_DIM
    averager = jnp.where(r_i == c_i, 1.0 / HEAD_DIM, 0.0).astype(BF16)

    def head_rms(x, w):
        sq = x * x
        hi = sq.astype(BF16)
        lo = (sq - hi.astype(F32)).astype(BF16)
        mean = (jnp.dot(hi, averager, preferred_element_type=F32)
                + jnp.dot(lo, averager, preferred_element_type=F32))
        return x * lax.rsqrt(mean + EPS) * w

    qn = head_rms(q_ref[...], qw_ref[...]) * (HEAD_DIM ** -0.5)
    ks[...] = head_rms(k_ref[...], kw_ref[...])
    v = v_ref[...]
    for hh in range(heads):
        mine = (lane // HEAD_DIM) == hh
        qs[hh] = jnp.where(mine, qn, 0.0)
        vs[hh] = jnp.where(mine, v, 1.0)

    for p, (_, dil) in enumerate(DILATED):
        nblk = s_len // dil // BLK
        shift = dil.bit_length() - 1

        def group(qstarts, first, p=p, dil=dil):
            nk = BLK if first else 2 * BLK
            chains = [(hh, q0) for q0 in qstarts for hh in range(heads)]
            rows = lambda q0, n: pl.ds(q0, n, stride=dil)
            logits = []
            for hh, q0 in chains:
                qb = qs[hh, rows(q0, BLK), :].astype(BF16)
                kb = ks[rows(q0 - (nk - BLK) * dil, nk), :].astype(BF16)
                s = lax.dot_general(qb, kb, (((1,), (1,)), ((), ())), preferred_element_type=F32)
                bias = bias_ref[p, hh, :, BLK:] if first else bias_ref[p, hh]
                logits.append(jnp.where(bias > 0.1 * NEG, s + bias, NEG))
            probs = []
            for (hh, q0), lg in zip(chains, logits):
                m = jnp.max(lg, axis=-1, keepdims=True)
                m_s[hh, p, rows(q0, BLK), :] = jnp.broadcast_to(m, (BLK, LANES))
                probs.append(jnp.exp(lg - m).astype(BF16))
            for (hh, q0), pe in zip(chains, probs):
                vb = vs[hh, rows(q0 - (nk - BLK) * dil, nk), :].astype(BF16)
                acc_s[hh, p, rows(q0, BLK), :] = jnp.dot(pe, vb, preferred_element_type=F32)

        def first_groups(g, carry, group=group):
            group([g * ATTN_GROUP + j for j in range(min(dil, ATTN_GROUP))], True)
            return carry

        def later_groups(g, carry, group=group, dil=dil, shift=shift):
            starts = []
            for j in range(ATTN_GROUP):
                i = dil + g * ATTN_GROUP + j
                starts.append(jnp.bitwise_and(i, dil - 1) + jnp.right_shift(i, shift) * (dil * BLK))
            group(starts, False)
            return carry

        lax.fori_loop(0, -(-dil // ATTN_GROUP), first_groups, 0)
        n_later = (nblk - 1) * dil
        lax.fori_loop(0, n_later // ATTN_GROUP, later_groups, 0)
        tail = [dil + n_later // ATTN_GROUP * ATTN_GROUP + j for j in range(n_later % ATTN_GROUP)]
        if tail:
            group([(i & (dil - 1)) + (i >> shift) * (dil * BLK) for i in tail], False)

    out = None
    for hh in range(heads):
        ms = [m_s[hh, p] for p in range(len(DILATED))]
        m_all = functools.reduce(jnp.maximum, ms)
        tot = sum(jnp.exp(m - m_all) * acc_s[hh, p] for p, m in enumerate(ms))
        ratio = tot / pltpu.roll(tot, HEAD_DIM, axis=1)
        out = ratio if out is None else jnp.where((lane // HEAD_DIM) == hh, ratio, out)
    o_ref[...] = out


def _attention(qkv3, q_norm_w, k_norm_w, bias):
    b, s, _ = qkv3.shape
    nhp = D_ATTN // LANES
    heads = LANES // HEAD_DIM
    npat = len(DILATED)
    col = lambda off: pl.BlockSpec((None, s, LANES), lambda bi, hp, off=off: (bi, 0, off + hp))
    return pl.pallas_call(
        _attn_kernel,
        grid=(b, nhp),
        in_specs=[col(0), col(nhp), col(2 * nhp),
                  pl.BlockSpec((1, LANES), lambda bi, hp: (0, 0)),
                  pl.BlockSpec((1, LANES), lambda bi, hp: (0, 0)),
                  pl.BlockSpec((npat, LANES // HEAD_DIM, BLK, 2 * BLK), lambda bi, hp: (0, hp, 0, 0))],
        out_specs=pl.BlockSpec((None, s, LANES), lambda bi, hp: (bi, 0, hp)),
        out_shape=jax.ShapeDtypeStruct((b, s, D_ATTN), F32),
        scratch_shapes=[pltpu.VMEM((heads, s, LANES), F32), pltpu.VMEM((s, LANES), F32),
                        pltpu.VMEM((heads, s, LANES), F32),
                        pltpu.VMEM((heads, npat, s, LANES), F32),
                        pltpu.VMEM((heads, npat, s, LANES), F32)],
        compiler_params=pltpu.CompilerParams(dimension_semantics=("arbitrary", "arbitrary"),
                                             vmem_limit_bytes=VMEM_LIMIT),
        name="attn",
    )(qkv3, qkv3, qkv3, jnp.tile(q_norm_w, (1, heads)), jnp.tile(k_norm_w, (1, heads)), bias)


def _out_proj_kernel(attn_ref, sgu_ref, x_ref, mixw_ref, wout_ref, ln2_ref, x1_ref, hnt_ref):
    an = _rms_rows(attn_ref[...], mixw_ref[...]).astype(BF16)
    y = jnp.dot(an, wout_ref[:D_ATTN, :], preferred_element_type=F32)
    y = y + jnp.dot(sgu_ref[...], wout_ref[D_ATTN:, :], preferred_element_type=F32)
    x1 = x_ref[...] + y
    x1_ref[...] = x1
    hnt_ref[...] = _rms_rows(x1, ln2_ref[...]).T.astype(BF16)


def _out_proj(attn2, sgu_n, x2, mixw_attn, w_out, ln2, tm):
    t, d = x2.shape
    full = lambda shape: pl.BlockSpec(shape, lambda i: (0,) * len(shape))
    return pl.pallas_call(
        _out_proj_kernel,
        grid=(t // tm,),
        in_specs=[pl.BlockSpec((tm, D_ATTN), lambda i: (i, 0)),
                  pl.BlockSpec((tm, D_SGU), lambda i: (i, 0)),
                  pl.BlockSpec((tm, d), lambda i: (i, 0)),
                  full((1, D_ATTN)), full((D_ATTN + D_SGU, d)), full((1, d))],
        out_specs=[pl.BlockSpec((tm, d), lambda i: (i, 0)),
                   pl.BlockSpec((d, tm), lambda i: (0, i))],
        out_shape=[jax.ShapeDtypeStruct((t, d), F32), jax.ShapeDtypeStruct((d, t), BF16)],
        compiler_params=pltpu.CompilerParams(dimension_semantics=("arbitrary",),
                                             vmem_limit_bytes=VMEM_LIMIT),
        name="out_proj",
    )(attn2, sgu_n, x2, mixw_attn, w_out, ln2)


def _extract16(x, order, tie_break):
    rank = jnp.full(x.shape, PEER_TOPK, jnp.int32)
    vals = []
    cur = x
    for r in range(PEER_TOPK):
        m = jnp.max(cur, axis=0, keepdims=True)
        sel = cur == m
        if tie_break:
            first = jnp.min(jnp.where(sel, order, jnp.iinfo(jnp.int32).max), axis=0, keepdims=True)
            sel = order == first
        rank = jnp.where(sel, r, rank)
        cur = jnp.where(sel, -jnp.inf, cur)
        vals.append(m)
    return rank, vals


def _taken(rank):
    return jnp.sum(jnp.where(rank < PEER_TOPK, 1.0, 0.0))


def _staircase_counts(v1, v2, tb, tie_break):
    v2_all = jnp.concatenate(v2, axis=0)
    cands, poss, sizes = [], [], []
    for r1 in range(PEER_TOPK):
        rows = PEER_TOPK if r1 == 0 else 8
        r2 = lax.broadcasted_iota(jnp.int32, (rows, tb), 0)
        cands.append(jnp.where(r2 < PEER_TOPK // (r1 + 1), v1[r1] + v2_all[:rows], -jnp.inf))
        poss.append(r2 + r1 * PEER_TOPK)
        sizes.append(rows)
    cand = jnp.concatenate(cands, axis=0)
    pos = jnp.concatenate(poss, axis=0)
    rank, _ = _extract16(cand, pos, tie_break)
    picked = jnp.where(rank < PEER_TOPK, 1.0, 0.0)
    z = jnp.sum(picked * jnp.exp(cand - (v1[0] + v2[0])), axis=0, keepdims=True)
    counts, lo = [], 0
    for rows in sizes:
        counts.append(jnp.sum(picked[lo:lo + rows], axis=0, keepdims=True))
        lo += rows
    return counts, z, _taken(rank)


def _head_codes(s1, s2, tie_break):
    tb = s1.shape[1]
    iota = lax.broadcasted_iota(jnp.int32, s1.shape, 0)
    rank1, v1 = _extract16(s1, iota, tie_break)
    rank2, v2 = _extract16(s2, iota, tie_break)
    counts, z, taken = _staircase_counts(v1, v2, tb, tie_break)
    c1 = jnp.zeros(s1.shape, F32)
    for r1 in range(PEER_TOPK):
        c1 = jnp.where(rank1 == r1, counts[r1], c1)
    q1 = jnp.exp(s1 - v1[0]) / z
    c2 = rank2.astype(F32).astype(BF16)
    p2 = jnp.exp(s2 - v2[0]).astype(BF16)
    return (c1, q1, c2, p2), taken + _taken(rank1) + _taken(rank2)


def _route_kernel(hnt_ref, wqt_ref, k1_ref, k2_ref, c1_ref, q1_ref, c2_ref, p2_ref, qt_s):
    tb = hnt_ref.shape[1]
    half = PEER_D_KEY // 2
    qt_s[...] = jnp.dot(wqt_ref[...], hnt_ref[...], preferred_element_type=F32)

    def scores(h):
        base = pl.multiple_of(h * PEER_D_KEY, PEER_D_KEY)
        qa = qt_s[pl.ds(base, half), :].astype(BF16)
        qb = qt_s[pl.ds(base + half, half), :].astype(BF16)
        return (jnp.dot(k1_ref[h], qa, preferred_element_type=F32),
                jnp.dot(k2_ref[h], qb, preferred_element_type=F32))

    def heads(g, carry):
        hs = [g * ROUTE_HEADS + j for j in range(ROUTE_HEADS)]
        ss = [scores(h) for h in hs]
        fast = [_head_codes(s1, s2, tie_break=False) for s1, s2 in ss]
        for h, (s1, s2), (codes, taken) in zip(hs, ss, fast):
            untied = taken == float(3 * PEER_TOPK * tb)
            codes = lax.cond(untied, lambda codes=codes: codes,
                             lambda s1=s1, s2=s2: _head_codes(s1, s2, tie_break=True)[0])
            for ref, code in zip((c1_ref, q1_ref, c2_ref, p2_ref), codes):
                ref[h] = code
        return carry

    lax.fori_loop(0, PEER_HEADS // ROUTE_HEADS, heads, 0)


def _route(hnt, wqt, keys1, keys2, tb):
    d, t = hnt.shape
    nq = wqt.shape[0]
    full = lambda shape: pl.BlockSpec(shape, lambda i: (0,) * len(shape))
    code = pl.BlockSpec((PEER_HEADS, PEER_KEYS, tb), lambda i: (0, 0, i))
    code_shape = lambda dt: jax.ShapeDtypeStruct((PEER_HEADS, PEER_KEYS, t), dt)
    return pl.pallas_call(
        _route_kernel,
        grid=(t // tb,),
        in_specs=[pl.BlockSpec((d, tb), lambda i: (0, i)), full((nq, d)),
                  full(keys1.shape), full(keys2.shape)],
        out_specs=[code] * 4,
        out_shape=[code_shape(F32), code_shape(F32), code_shape(BF16), code_shape(BF16)],
        scratch_shapes=[pltpu.VMEM((nq, tb), F32)],
        compiler_params=pltpu.CompilerParams(dimension_semantics=("arbitrary",),
                                             vmem_limit_bytes=VMEM_LIMIT),
        name="route",
    )(hnt, wqt, keys1, keys2)


def _experts_kernel(hnt_ref, u_ref, vt_ref, c1_ref, q1_ref, c2_ref, p2_ref, x1_ref, o_ref,
                    acc_s, st_s, z_s, hnt_s, c2_s, p2_s, *, n_eb, n_blocks):
    t = pl.program_id(0)
    eb = u_ref.shape[0]
    tb = hnt_ref.shape[1]
    keys_per_block = eb // PEER_KEYS
    j_b = jnp.clip(t - 1, 0, n_blocks - 1) % n_eb
    j_c = jnp.clip(t - 2, 0, n_blocks - 1) % n_eb

    @pl.when(t == 0)
    def _():
        st_s[...] = jnp.zeros_like(st_s)
        z_s[...] = jnp.zeros_like(z_s)
        acc_s[...] = jnp.zeros_like(acc_s)

    @pl.when(jnp.logical_and(t >= 2, j_c == 0))
    def _():
        acc_s[...] = jnp.zeros_like(acc_s)

    @pl.when(jnp.minimum(t, n_blocks - 1) % n_eb == 0)
    def _():
        hnt_s[...] = hnt_ref[...]

    @pl.when(j_b == 0)
    def _():
        c2_s[...] = c2_ref[...]
        p2_s[...] = p2_ref[...]

    row0 = (j_b * keys_per_block) % CODE_ROW_BLOCK
    rows_of = lambda ref, ii: [ref[h, pl.ds(row0 + ii, 1), :] for h in range(PEER_HEADS)]
    c1_all = [rows_of(c1_ref, ii) for ii in range(keys_per_block)]
    q1_all = [rows_of(q1_ref, ii) for ii in range(keys_per_block)]
    packed = (PEER_KEYS // BF16_ROWS, BF16_ROWS, LANES)
    row_bf16 = lambda r: jnp.broadcast_to(r, (BF16_ROWS, LANES)).astype(BF16)[None]

    def stage_a(cols):
        st_s[:, cols] = jnp.dot(u_ref[...], hnt_s[:, cols], preferred_element_type=F32)

    def stage_b(ii, cols):
        rows = slice(ii * PEER_KEYS, (ii + 1) * PEER_KEYS)
        gate = None
        for h in range(PEER_HEADS):
            c1 = row_bf16(c1_all[ii][h][:, cols])
            q1 = row_bf16(q1_all[ii][h][:, cols])
            term = jnp.where(c2_s[h, :, cols].reshape(packed) < c1,
                             p2_s[h, :, cols].reshape(packed) * q1, jnp.zeros((), BF16))
            gate = term if gate is None else gate + term
        act = jax.nn.gelu(st_s[rows, cols]).astype(BF16).reshape(packed)
        z_s[rows, cols] = (act * gate).reshape(PEER_KEYS, LANES)

    def stage_c(cols):
        acc_s[:, cols] += jnp.dot(vt_ref[...], z_s[:, cols], preferred_element_type=F32)

    for mc in range(tb // MXU_GROUP_COLS):
        stage_c(slice(mc * MXU_GROUP_COLS, (mc + 1) * MXU_GROUP_COLS))
        for lc in range(MXU_GROUP_COLS // LANES):
            lo = mc * MXU_GROUP_COLS + lc * LANES
            for ii in range(keys_per_block):
                stage_b(ii, slice(lo, lo + LANES))
        stage_a(slice(mc * MXU_GROUP_COLS, (mc + 1) * MXU_GROUP_COLS))

    @pl.when(jnp.logical_and(t >= 2, j_c == n_eb - 1))
    def _():
        o_ref[...] = x1_ref[...] + acc_s[...].T


def _experts(hnt, u_tab, vt_blocks, codes, x1, tb):
    d, t = hnt.shape
    n_eb, _, eb = vt_blocks.shape
    keys_per_block = eb // PEER_KEYS
    assert CODE_ROW_BLOCK % keys_per_block == 0
    n_blocks = (t // tb) * n_eb
    g_a = lambda s: jnp.minimum(s, n_blocks - 1)
    g_b = lambda s: jnp.clip(s - 1, 0, n_blocks - 1)
    g_c = lambda s: jnp.clip(s - 2, 0, n_blocks - 1)
    code_rows = pl.BlockSpec(
        (PEER_HEADS, CODE_ROW_BLOCK, tb),
        lambda s: (0, (g_b(s) % n_eb) * keys_per_block // CODE_ROW_BLOCK, g_b(s) // n_eb))
    code_keys = pl.BlockSpec((PEER_HEADS, PEER_KEYS, tb), lambda s: (0, 0, g_b(s) // n_eb))
    assert codes[0].dtype == F32 and codes[1].dtype == F32
    assert codes[2].dtype == BF16 and codes[3].dtype == BF16
    return pl.pallas_call(
        functools.partial(_experts_kernel, n_eb=n_eb, n_blocks=n_blocks),
        grid=(n_blocks + 2,),
        in_specs=[pl.BlockSpec((d, tb), lambda s: (0, g_a(s) // n_eb)),
                  pl.BlockSpec((eb, d), lambda s: (g_a(s) % n_eb, 0)),
                  pl.BlockSpec((None, d, eb), lambda s: (g_c(s) % n_eb, 0, 0)),
                  code_rows, code_rows, code_keys, code_keys,
                  pl.BlockSpec((tb, d), lambda s: (g_c(s) // n_eb, 0))],
        out_specs=pl.BlockSpec((tb, d), lambda s: (g_c(s) // n_eb, 0)),
        out_shape=jax.ShapeDtypeStruct((t, d), F32),
        scratch_shapes=[pltpu.VMEM((d, tb), F32), pltpu.VMEM((eb, tb), F32),
                        pltpu.VMEM((eb, tb), BF16), pltpu.VMEM((d, tb), BF16),
                        pltpu.VMEM((PEER_HEADS, PEER_KEYS, tb), BF16),
                        pltpu.VMEM((PEER_HEADS, PEER_KEYS, tb), BF16)],
        compiler_params=pltpu.CompilerParams(dimension_semantics=("arbitrary",),
                                             vmem_limit_bytes=VMEM_LIMIT),
        name="experts",
    )(hnt, u_tab, vt_blocks, *codes, x1)


def kernel(x, ln1_w, w_in, q_norm_w, k_norm_w, sgu_norm_w, sgu_w, sgu_b, mix_norm_w, w_out,
           ln2_w, peer_w_query, peer_keys1, peer_keys2, peer_u, peer_v, rel_bias):
    b, s, d = x.shape
    t = b * s
    bias = _bias_tables(rel_bias)
    for l in range(ln1_w.shape[0]):
        x2 = x.reshape(t, d)
        qkv, sgu_n = _in_proj(x2, ln1_w[l][None], w_in[l].astype(BF16), sgu_norm_w[l], sgu_w[l],
                              sgu_b[l], mix_norm_w[l, D_ATTN:][None], tm=512)
        attn = _attention(qkv.reshape(b, s, 3 * D_ATTN), q_norm_w[l][None], k_norm_w[l][None], bias)
        x1, hnt = _out_proj(attn.reshape(t, D_ATTN), sgu_n, x2, mix_norm_w[l, :D_ATTN][None],
                            w_out[l].astype(BF16), ln2_w[l][None], tm=512)
        codes = _route(hnt, peer_w_query[l].T.astype(BF16), peer_keys1[l].astype(BF16),
                       peer_keys2[l].astype(BF16), tb=256)
        vt_blocks = peer_v[l].reshape(-1, EXPERT_BLOCK, d).transpose(0, 2, 1).astype(BF16)
        out = _experts(hnt, peer_u[l].astype(BF16), vt_blocks, codes, x1, tb=EXPERT_TOKENS)
        x = out.reshape(b, s, d)
    return x
```

```python
import functools
import math

import numpy as np
import jax
import jax.numpy as jnp
from jax import lax
from jax.experimental import pallas as pl
from jax.experimental.pallas import tpu as pltpu

F32 = jnp.float32
BF16 = jnp.bfloat16

ATTN_HEADS = 8
HEAD_DIM = 64
D_ATTN = ATTN_HEADS * HEAD_DIM
SGU_GROUPS = 4
SGU_CH = 128
D_SGU = SGU_GROUPS * SGU_CH
CHUNK = 128
DILATED = ((128, 1), (512, 4), (2048, 16))
BLK = 128
NUM_BUCKETS = 32
MAX_DISTANCE = 2048
PEER_HEADS = 8
PEER_KEYS = 128
PEER_D_KEY = 256
PEER_TOPK = 16
EPS = 1e-6
NEG = -1e30

LANES = 128
BF16_ROWS = 16
MXU_COLS = 256
CODE_ROW_BLOCK = 8
ATTN_GROUP = 4
ROUTE_HEADS = 2
GATE_ROWS_PER_PASS = 2
EXPERT_BLOCK = 1024
EXPERT_TOKENS = 512
VMEM_LIMIT = 56 * 1024 * 1024


def _twice_gelu(x):
    a = math.sqrt(2.0 / math.pi)
    return x * (1.0 + jnp.tanh(x * (a + (a * 0.044715) * (x * x))))


def _rms_rows(x, w):
    return x * lax.rsqrt(jnp.mean(x * x, axis=-1, keepdims=True) + EPS) * w


def _bucket_tables():
    qi = np.arange(BLK)[:, None]
    kj = np.arange(2 * BLK)[None, :]
    rel = BLK + qi - kj
    max_exact = NUM_BUCKETS // 2
    buckets, valid = [], []
    for window, dil in DILATED:
        w_sub = window // dil
        dist = np.maximum(rel, 0) * dil
        d_f = np.maximum(dist, max_exact).astype(np.float32)
        large = max_exact + (np.log(d_f / np.float32(max_exact)) / np.float32(math.log(MAX_DISTANCE / max_exact))
                             * np.float32(NUM_BUCKETS - max_exact)).astype(np.int32)
        large = np.minimum(large, NUM_BUCKETS - 1)
        buckets.append(np.where(dist < max_exact, dist, large).astype(np.int32))
        valid.append(((rel >= 0) & (rel <= w_sub)).astype(np.int32))
    return np.stack(buckets), np.stack(valid)


def _bias_kernel(rb_ref, bkt_ref, valid_ref, o_ref):
    h = pl.program_id(1)
    bkt = bkt_ref[...]
    acc = jnp.zeros(bkt.shape, F32)
    for j in range(NUM_BUCKETS):
        acc = jnp.where(bkt == j, rb_ref[j, h], acc)
    o_ref[...] = jnp.where(valid_ref[...] > 0, acc, NEG)


def _bias_tables(rel_bias):
    bkt, valid = _bucket_tables()
    npat = len(DILATED)
    blk = pl.BlockSpec((None, BLK, 2 * BLK), lambda p, h: (p, 0, 0))
    return pl.pallas_call(
        _bias_kernel,
        grid=(npat, ATTN_HEADS),
        in_specs=[pl.BlockSpec(memory_space=pltpu.SMEM), blk, blk],
        out_specs=pl.BlockSpec((None, None, BLK, 2 * BLK), lambda p, h: (p, h, 0, 0)),
        out_shape=jax.ShapeDtypeStruct((npat, ATTN_HEADS, BLK, 2 * BLK), F32),
        name="bias",
    )(rel_bias, jnp.asarray(bkt), jnp.asarray(valid))


def _in_proj_kernel(x_ref, ln1_ref, win_ref, sgnw_ref, sguw_ref, bst_ref, mixw_ref,
                    qkv_ref, sgu_ref, sgu_s):
    tm = x_ref.shape[0]
    h = _rms_rows(x_ref[...], ln1_ref[...]).astype(BF16)
    proj = jnp.dot(h, win_ref[...], preferred_element_type=F32)
    qkv_ref[...] = proj[:, :3 * D_ATTN]
    row = lax.broadcasted_iota(jnp.int32, (CHUNK, CHUNK), 0)
    col = lax.broadcasted_iota(jnp.int32, (CHUNK, CHUNK), 1)
    for g in range(SGU_GROUPS):
        u_lo = 3 * D_ATTN + g * SGU_CH
        v_lo = 3 * D_ATTN + D_SGU + g * SGU_CH
        ua = jax.nn.gelu(proj[:, u_lo:u_lo + SGU_CH])
        va = jax.nn.gelu(proj[:, v_lo:v_lo + SGU_CH])
        vn = _rms_rows(va, sgnw_ref[g:g + 1, :]).astype(BF16)
        w = jnp.where(row >= col, sguw_ref[g], 0.0).astype(BF16)
        bias = bst_ref[:, g:g + 1]
        for c in range(tm // CHUNK):
            rows = slice(c * CHUNK, (c + 1) * CHUNK)
            spatial = jnp.dot(w, vn[rows], preferred_element_type=F32) + bias
            sgu_s[rows, g * SGU_CH:(g + 1) * SGU_CH] = ua[rows] * spatial
    sgu_ref[...] = _rms_rows(sgu_s[...], mixw_ref[...]).astype(BF16)


def _in_proj(x2, ln1, w_in, sgu_norm_w, sgu_w, sgu_b, mixw_sgu, tm):
    t, d = x2.shape
    dproj = w_in.shape[1]
    full = lambda shape: pl.BlockSpec(shape, lambda i: (0,) * len(shape))
    return pl.pallas_call(
        _in_proj_kernel,
        grid=(t // tm,),
        in_specs=[pl.BlockSpec((tm, d), lambda i: (i, 0)),
                  full((1, d)), full((d, dproj)), full((SGU_GROUPS, SGU_CH)),
                  full((SGU_GROUPS, CHUNK, CHUNK)), full((CHUNK, SGU_GROUPS)), full((1, D_SGU))],
        out_specs=[pl.BlockSpec((tm, 3 * D_ATTN), lambda i: (i, 0)),
                   pl.BlockSpec((tm, D_SGU), lambda i: (i, 0))],
        out_shape=[jax.ShapeDtypeStruct((t, 3 * D_ATTN), F32),
                   jax.ShapeDtypeStruct((t, D_SGU), BF16)],
        scratch_shapes=[pltpu.VMEM((tm, D_SGU), F32)],
        compiler_params=pltpu.CompilerParams(dimension_semantics=("arbitrary",),
                                             vmem_limit_bytes=VMEM_LIMIT),
        name="in_proj",
    )(x2, ln1, w_in, sgu_norm_w, sgu_w, sgu_b.T, mixw_sgu)


def _attn_kernel(q_ref, k_ref, v_ref, qw_ref, kw_ref, bias_ref, o_ref,
                 qs, ks, vs, acc_s, m_s):
    s_len = q_ref.shape[0]
    heads = LANES // HEAD_DIM
    lane = lax.broadcasted_iota(jnp.int32, (s_len, LANES), 1)
    r_i = lax.broadcasted_iota(jnp.int32, (LANES, LANES), 0) // HEAD_DIM
    c_i = lax.broadcasted_iota(jnp.int32, (LANES, LANES), 1) // HEAD_DIM
    averager = jnp.where(r_i == c_i, 1.0 / HEAD_DIM, 0.0).astype(BF16)

    def head_rms(x, w):
        sq = x * x
        hi = sq.astype(BF16)
        lo = (sq - hi.astype(F32)).astype(BF16)
        mean = (jnp.dot(hi, averager, preferred_element_type=F32)
                + jnp.dot(lo, averager, preferred_element_type=F32))
        return x * lax.rsqrt(mean + EPS) * w

    qn = head_rms(q_ref[...], qw_ref[...]) * (HEAD_DIM ** -0.5)
    ks[...] = head_rms(k_ref[...], kw_ref[...])
    v = v_ref[...]
    for hh in range(heads):
        mine = (lane // HEAD_DIM) == hh
        qs[hh] = jnp.where(mine, qn, 0.0)
        vs[hh] = jnp.where(mine, v, 1.0)

    for p, (_, dil) in enumerate(DILATED):
        nblk = s_len // dil // BLK
        shift = dil.bit_length() - 1

        def group(qstarts, first, p=p, dil=dil):
            nk = BLK if first else 2 * BLK
            chains = [(hh, q0) for q0 in qstarts for hh in range(heads)]
            rows = lambda q0, n: pl.ds(q0, n, stride=dil)
            logits = []
            for hh, q0 in chains:
                qb = qs[hh, rows(q0, BLK), :].astype(BF16)
                kb = ks[rows(q0 - (nk - BLK) * dil, nk), :].astype(BF16)
                s = lax.dot_general(qb, kb, (((1,), (1,)), ((), ())), preferred_element_type=F32)
                bias = bias_ref[p, hh, :, BLK:] if first else bias_ref[p, hh]
                logits.append(jnp.where(bias > 0.1 * NEG, s + bias, NEG))
            probs = []
            for (hh, q0), lg in zip(chains, logits):
                m = jnp.max(lg, axis=-1, keepdims=True)
                m_s[hh, p, rows(q0, BLK), :] = jnp.broadcast_to(m, (BLK, LANES))
                probs.append(jnp.exp(lg - m).astype(BF16))
            for (hh, q0), pe in zip(chains, probs):
                vb = vs[hh, rows(q0 - (nk - BLK) * dil, nk), :].astype(BF16)
                acc_s[hh, p, rows(q0, BLK), :] = jnp.dot(pe, vb, preferred_element_type=F32)

        def first_groups(g, carry, group=group):
            group([g * ATTN_GROUP + j for j in range(min(dil, ATTN_GROUP))], True)
            return carry

        def later_groups(g, carry, group=group, dil=dil, shift=shift):
            starts = []
            for j in range(ATTN_GROUP):
                i = dil + g * ATTN_GROUP + j
                starts.append(jnp.bitwise_and(i, dil - 1) + jnp.right_shift(i, shift) * (dil * BLK))
            group(starts, False)
            return carry

        lax.fori_loop(0, -(-dil // ATTN_GROUP), first_groups, 0)
        n_later = (nblk - 1) * dil
        lax.fori_loop(0, n_later // ATTN_GROUP, later_groups, 0)
        tail = [dil + n_later // ATTN_GROUP * ATTN_GROUP + j for j in range(n_later % ATTN_GROUP)]
        if tail:
            group([(i & (dil - 1)) + (i >> shift) * (dil * BLK) for i in tail], False)

    out = None
    for hh in range(heads):
        ms = [m_s[hh, p] for p in range(len(DILATED))]
        m_all = functools.reduce(jnp.maximum, ms)
        tot = sum(jnp.exp(m - m_all) * acc_s[hh, p] for p, m in enumerate(ms))
        ratio = tot / pltpu.roll(tot, HEAD_DIM, axis=1)
        out = ratio if out is None else jnp.where((lane // HEAD_DIM) == hh, ratio, out)
    o_ref[...] = out


def _attention(qkv3, q_norm_w, k_norm_w, bias):
    b, s, _ = qkv3.shape
    nhp = D_ATTN // LANES
    heads = LANES // HEAD_DIM
    npat = len(DILATED)
    col = lambda off: pl.BlockSpec((None, s, LANES), lambda bi, hp, off=off: (bi, 0, off + hp))
    return pl.pallas_call(
        _attn_kernel,
        grid=(b, nhp),
        in_specs=[col(0), col(nhp), col(2 * nhp),
                  pl.BlockSpec((1, LANES), lambda bi, hp: (0, 0)),
                  pl.BlockSpec((1, LANES), lambda bi, hp: (0, 0)),
                  pl.BlockSpec((npat, LANES // HEAD_DIM, BLK, 2 * BLK), lambda bi, hp: (0, hp, 0, 0))],
        out_specs=pl.BlockSpec((None, s, LANES), lambda bi, hp: (bi, 0, hp)),
        out_shape=jax.ShapeDtypeStruct((b, s, D_ATTN), F32),
        scratch_shapes=[pltpu.VMEM((heads, s, LANES), F32), pltpu.VMEM((s, LANES), F32),
                        pltpu.VMEM((heads, s, LANES), F32),
                        pltpu.VMEM((heads, npat, s, LANES), F32),
                        pltpu.VMEM((heads, npat, s, LANES), F32)],
        compiler_params=pltpu.CompilerParams(dimension_semantics=("arbitrary", "arbitrary"),
                                             vmem_limit_bytes=VMEM_LIMIT),
        name="attn",
    )(qkv3, qkv3, qkv3, jnp.tile(q_norm_w, (1, heads)), jnp.tile(k_norm_w, (1, heads)), bias)


def _out_proj_kernel(attn_ref, sgu_ref, x_ref, mixw_ref, wout_ref, ln2_ref, x1_ref, hnt_ref):
    an = _rms_rows(attn_ref[...], mixw_ref[...]).astype(BF16)
    y = jnp.dot(an, wout_ref[:D_ATTN, :], preferred_element_type=F32)
    y = y + jnp.dot(sgu_ref[...], wout_ref[D_ATTN:, :], preferred_element_type=F32)
    x1 = x_ref[...] + y
    x1_ref[...] = x1
    hnt_ref[...] = _rms_rows(x1, ln2_ref[...]).T.astype(BF16)


def _out_proj(attn2, sgu_n, x2, mixw_attn, w_out, ln2, tm):
    t, d = x2.shape
    full = lambda shape: pl.BlockSpec(shape, lambda i: (0,) * len(shape))
    return pl.pallas_call(
        _out_proj_kernel,
        grid=(t // tm,),
        in_specs=[pl.BlockSpec((tm, D_ATTN), lambda i: (i, 0)),
                  pl.BlockSpec((tm, D_SGU), lambda i: (i, 0)),
                  pl.BlockSpec((tm, d), lambda i: (i, 0)),
                  full((1, D_ATTN)), full((D_ATTN + D_SGU, d)), full((1, d))],
        out_specs=[pl.BlockSpec((tm, d), lambda i: (i, 0)),
                   pl.BlockSpec((d, tm), lambda i: (0, i))],
        out_shape=[jax.ShapeDtypeStruct((t, d), F32), jax.ShapeDtypeStruct((d, t), BF16)],
        compiler_params=pltpu.CompilerParams(dimension_semantics=("arbitrary",),
                                             vmem_limit_bytes=VMEM_LIMIT),
        name="out_proj",
    )(attn2, sgu_n, x2, mixw_attn, w_out, ln2)


def _extract16(x, order, tie_break):
    rank = jnp.full(x.shape, PEER_TOPK, jnp.int32)
    vals = []
    cur = x
    for r in range(PEER_TOPK):
        m = jnp.max(cur, axis=0, keepdims=True)
        sel = cur == m
        if tie_break:
            first = jnp.min(jnp.where(sel, order, jnp.iinfo(jnp.int32).max), axis=0, keepdims=True)
            sel = order == first
        rank = jnp.where(sel, r, rank)
        cur = jnp.where(sel, -jnp.inf, cur)
        vals.append(m)
    return rank, vals


def _taken(rank):
    return jnp.sum(jnp.where(rank < PEER_TOPK, 1.0, 0.0))


def _staircase_counts(v1, v2, tb, tie_break):
    v2_all = jnp.concatenate(v2, axis=0)
    cands, poss, sizes = [], [], []
    for r1 in range(PEER_TOPK):
        rows = PEER_TOPK if r1 == 0 else 8
        r2 = lax.broadcasted_iota(jnp.int32, (rows, tb), 0)
        cands.append(jnp.where(r2 < PEER_TOPK // (r1 + 1), v1[r1] + v2_all[:rows], -jnp.inf))
        poss.append(r2 + r1 * PEER_TOPK)
        sizes.append(rows)
    cand = jnp.concatenate(cands, axis=0)
    pos = jnp.concatenate(poss, axis=0)
    rank, _ = _extract16(cand, pos, tie_break)
    picked = jnp.where(rank < PEER_TOPK, 1.0, 0.0)
    z = jnp.sum(picked * jnp.exp(cand - (v1[0] + v2[0])), axis=0, keepdims=True)
    counts, lo = [], 0
    for rows in sizes:
        counts.append(jnp.sum(picked[lo:lo + rows], axis=0, keepdims=True))
        lo += rows
    return counts, z, _taken(rank)


def _head_codes(s1, s2, tie_break):
    tb = s1.shape[1]
    iota = lax.broadcasted_iota(jnp.int32, s1.shape, 0)
    rank1, v1 = _extract16(s1, iota, tie_break)
    rank2, v2 = _extract16(s2, iota, tie_break)
    counts, z, taken = _staircase_counts(v1, v2, tb, tie_break)
    c1 = jnp.zeros(s1.shape, F32)
    for r1 in range(PEER_TOPK):
        c1 = jnp.where(rank1 == r1, counts[r1], c1)
    q1 = jnp.exp(s1 - v1[0]) * (0.5 / z)
    c2 = rank2.astype(F32).astype(BF16)
    p2 = jnp.exp(s2 - v2[0]).astype(BF16)
    return (c1, q1, c2, p2), taken + _taken(rank1) + _taken(rank2)


def _route_kernel(hnt_ref, wqt_ref, k1_ref, k2_ref, c1_ref, q1_ref, c2_ref, p2_ref, qt_s):
    tb = hnt_ref.shape[1]
    half = PEER_D_KEY // 2
    qt_s[...] = jnp.dot(wqt_ref[...], hnt_ref[...], preferred_element_type=F32)

    def scores(h):
        base = pl.multiple_of(h * PEER_D_KEY, PEER_D_KEY)
        qa = qt_s[pl.ds(base, half), :].astype(BF16)
        qb = qt_s[pl.ds(base + half, half), :].astype(BF16)
        return (jnp.dot(k1_ref[h], qa, preferred_element_type=F32),
                jnp.dot(k2_ref[h], qb, preferred_element_type=F32))

    def heads(g, carry):
        hs = [g * ROUTE_HEADS + j for j in range(ROUTE_HEADS)]
        ss = [scores(h) for h in hs]
        fast = [_head_codes(s1, s2, tie_break=False) for s1, s2 in ss]
        for h, (s1, s2), (codes, taken) in zip(hs, ss, fast):
            untied = taken == float(3 * PEER_TOPK * tb)
            codes = lax.cond(untied, lambda codes=codes: codes,
                             lambda s1=s1, s2=s2: _head_codes(s1, s2, tie_break=True)[0])
            for ref, code in zip((c1_ref, q1_ref, c2_ref, p2_ref), codes):
                ref[h] = code
        return carry

    lax.fori_loop(0, PEER_HEADS // ROUTE_HEADS, heads, 0)


def _route(hnt, wqt, keys1, keys2, tb):
    d, t = hnt.shape
    nq = wqt.shape[0]
    full = lambda shape: pl.BlockSpec(shape, lambda i: (0,) * len(shape))
    code = pl.BlockSpec((PEER_HEADS, PEER_KEYS, tb), lambda i: (0, 0, i))
    code_shape = lambda dt: jax.ShapeDtypeStruct((PEER_HEADS, PEER_KEYS, t), dt)
    return pl.pallas_call(
        _route_kernel,
        grid=(t // tb,),
        in_specs=[pl.BlockSpec((d, tb), lambda i: (0, i)), full((nq, d)),
                  full(keys1.shape), full(keys2.shape)],
        out_specs=[code] * 4,
        out_shape=[code_shape(F32), code_shape(F32), code_shape(BF16), code_shape(BF16)],
        scratch_shapes=[pltpu.VMEM((nq, tb), F32)],
        compiler_params=pltpu.CompilerParams(dimension_semantics=("arbitrary",),
                                             vmem_limit_bytes=VMEM_LIMIT),
        name="route",
    )(hnt, wqt, keys1, keys2)


def _experts_kernel(hnt_ref, u_ref, vt_ref, c1_ref, q1_ref, c2_ref, p2_ref, x1_ref, o_ref,
                    acc_s, st_s, z_s, hnt_s, c2_s, p2_s, *, n_eb, n_blocks):
    t = pl.program_id(0)
    eb = u_ref.shape[0]
    tb = hnt_ref.shape[1]
    keys_per_block = eb // PEER_KEYS
    j_b = jnp.clip(t - 1, 0, n_blocks - 1) % n_eb
    j_c = jnp.clip(t - 2, 0, n_blocks - 1) % n_eb

    @pl.when(t == 0)
    def _():
        st_s[...] = jnp.zeros_like(st_s)
        z_s[...] = jnp.zeros_like(z_s)
        acc_s[...] = jnp.zeros_like(acc_s)

    @pl.when(jnp.logical_and(t >= 2, j_c == 0))
    def _():
        acc_s[...] = jnp.zeros_like(acc_s)

    @pl.when(jnp.minimum(t, n_blocks - 1) % n_eb == 0)
    def _():
        hnt_s[...] = hnt_ref[...]

    @pl.when(j_b == 0)
    def _():
        c2_s[...] = c2_ref[...]
        p2_s[...] = p2_ref[...]

    row0 = (j_b * keys_per_block) % CODE_ROW_BLOCK
    rows_of = lambda ref, ii: [ref[h, pl.ds(row0 + ii, 1), :] for h in range(PEER_HEADS)]
    c1_all = [rows_of(c1_ref, ii) for ii in range(keys_per_block)]
    q1_all = [rows_of(q1_ref, ii) for ii in range(keys_per_block)]
    packed = (PEER_KEYS // BF16_ROWS, BF16_ROWS, LANES)
    row_bf16 = lambda r: jnp.broadcast_to(r, (BF16_ROWS, LANES)).astype(BF16)[None]

    def stage_a(cols):
        st_s[:, cols] = jnp.dot(u_ref[...], hnt_s[:, cols], preferred_element_type=F32)

    def stage_b(iis, cols):
        gates = [None] * len(iis)
        for h in range(PEER_HEADS):
            c2 = c2_s[h, :, cols].reshape(packed)
            p2 = p2_s[h, :, cols].reshape(packed)
            for n, ii in enumerate(iis):
                c1 = row_bf16(c1_all[ii][h][:, cols])
                q1 = row_bf16(q1_all[ii][h][:, cols])
                term = jnp.where(c2 < c1, p2 * q1, jnp.zeros((), BF16))
                gates[n] = term if gates[n] is None else gates[n] + term
        for n, ii in enumerate(iis):
            rows = slice(ii * PEER_KEYS, (ii + 1) * PEER_KEYS)
            act = _twice_gelu(st_s[rows, cols]).astype(BF16).reshape(packed)
            z_s[rows, cols] = (act * gates[n]).reshape(PEER_KEYS, LANES)

    def stage_c(cols):
        acc_s[:, cols] += jnp.dot(vt_ref[...], z_s[:, cols], preferred_element_type=F32)

    for mc in range(tb // MXU_COLS):
        stage_c(slice(mc * MXU_COLS, (mc + 1) * MXU_COLS))
        for lc in range(MXU_COLS // LANES):
            lo = mc * MXU_COLS + lc * LANES
            for ii in range(0, keys_per_block, GATE_ROWS_PER_PASS):
                stage_b(list(range(ii, ii + GATE_ROWS_PER_PASS)), slice(lo, lo + LANES))
        stage_a(slice(mc * MXU_COLS, (mc + 1) * MXU_COLS))

    @pl.when(jnp.logical_and(t >= 2, j_c == n_eb - 1))
    def _():
        o_ref[...] = x1_ref[...] + acc_s[...].T


def _experts(hnt, u_tab, vt_blocks, codes, x1, tb):
    d, t = hnt.shape
    n_eb, _, eb = vt_blocks.shape
    keys_per_block = eb // PEER_KEYS
    assert CODE_ROW_BLOCK % keys_per_block == 0
    n_blocks = (t // tb) * n_eb
    g_a = lambda s: jnp.minimum(s, n_blocks - 1)
    g_b = lambda s: jnp.clip(s - 1, 0, n_blocks - 1)
    g_c = lambda s: jnp.clip(s - 2, 0, n_blocks - 1)
    code_rows = pl.BlockSpec(
        (PEER_HEADS, CODE_ROW_BLOCK, tb),
        lambda s: (0, (g_b(s) % n_eb) * keys_per_block // CODE_ROW_BLOCK, g_b(s) // n_eb))
    code_keys = pl.BlockSpec((PEER_HEADS, PEER_KEYS, tb), lambda s: (0, 0, g_b(s) // n_eb))
    assert codes[0].dtype == F32 and codes[1].dtype == F32
    assert codes[2].dtype == BF16 and codes[3].dtype == BF16
    return pl.pallas_call(
        functools.partial(_experts_kernel, n_eb=n_eb, n_blocks=n_blocks),
        grid=(n_blocks + 2,),
        in_specs=[pl.BlockSpec((d, tb), lambda s: (0, g_a(s) // n_eb)),
                  pl.BlockSpec((eb, d), lambda s: (g_a(s) % n_eb, 0)),
                  pl.BlockSpec((None, d, eb), lambda s: (g_c(s) % n_eb, 0, 0)),
                  code_rows, code_rows, code_keys, code_keys,
                  pl.BlockSpec((tb, d), lambda s: (g_c(s) // n_eb, 0))],
        out_specs=pl.BlockSpec((tb, d), lambda s: (g_c(s) // n_eb, 0)),
        out_shape=jax.ShapeDtypeStruct((t, d), F32),
        scratch_shapes=[pltpu.VMEM((d, tb), F32), pltpu.VMEM((eb, tb), F32),
                        pltpu.VMEM((eb, tb), BF16), pltpu.VMEM((d, tb), BF16),
                        pltpu.VMEM((PEER_HEADS, PEER_KEYS, tb), BF16),
                        pltpu.VMEM((PEER_HEADS, PEER_KEYS, tb), BF16)],
        compiler_params=pltpu.CompilerParams(dimension_semantics=("arbitrary",),
                                             vmem_limit_bytes=VMEM_LIMIT),
        name="experts",
    )(hnt, u_tab, vt_blocks, *codes, x1)


def kernel(x, ln1_w, w_in, q_norm_w, k_norm_w, sgu_norm_w, sgu_w, sgu_b, mix_norm_w, w_out,
           ln2_w, peer_w_query, peer_keys1, peer_keys2, peer_u, peer_v, rel_bias):
    b, s, d = x.shape
    t = b * s
    bias = _bias_tables(rel_bias)
    for l in range(ln1_w.shape[0]):
        x2 = x.reshape(t, d)
        qkv, sgu_n = _in_proj(x2, ln1_w[l][None], w_in[l].astype(BF16), sgu_norm_w[l], sgu_w[l],
                              sgu_b[l], mix_norm_w[l, D_ATTN:][None], tm=512)
        attn = _attention(qkv.reshape(b, s, 3 * D_ATTN), q_norm_w[l][None], k_norm_w[l][None], bias)
        x1, hnt = _out_proj(attn.reshape(t, D_ATTN), sgu_n, x2, mix_norm_w[l, :D_ATTN][None],
                            w_out[l].astype(BF16), ln2_w[l][None], tm=512)
        codes = _route(hnt, peer_w_query[l].T.astype(BF16), peer_keys1[l].astype(BF16),
                       peer_keys2[l].astype(BF16), tb=256)
        vt_blocks = peer_v[l].reshape(-1, EXPERT_BLOCK, d).transpose(0, 2, 1).astype(BF16)
        out = _experts(hnt, peer_u[l].astype(BF16), vt_blocks, codes, x1, tb=EXPERT_TOKENS)
        x = out.reshape(b, s, d)
    return x
```

```python
import functools
import math

import numpy as np
import jax
import jax.numpy as jnp
from jax import lax
from jax.experimental import pallas as pl
from jax.experimental.pallas import tpu as pltpu

F32 = jnp.float32
BF16 = jnp.bfloat16

ATTN_HEADS = 8
HEAD_DIM = 64
D_ATTN = ATTN_HEADS * HEAD_DIM
SGU_GROUPS = 4
SGU_CH = 128
D_SGU = SGU_GROUPS * SGU_CH
CHUNK = 128
DILATED = ((128, 1), (512, 4), (2048, 16))
BLK = 128
NUM_BUCKETS = 32
MAX_DISTANCE = 2048
PEER_HEADS = 8
PEER_KEYS = 128
PEER_D_KEY = 256
PEER_TOPK = 16
EPS = 1e-6
NEG = -1e30

LANES = 128
BF16_ROWS = 16
MXU_COLS = 256
CODE_ROW_BLOCK = 8
ATTN_GROUP = 4
ROUTE_HEADS = 2
GATE_ROWS_PER_PASS = 2
EXPERT_BLOCK = 1024
EXPERT_TOKENS = 512
VMEM_LIMIT = 56 * 1024 * 1024


def _rms_rows(x, w):
    return x * lax.rsqrt(jnp.mean(x * x, axis=-1, keepdims=True) + EPS) * w


def _bucket_tables():
    qi = np.arange(BLK)[:, None]
    kj = np.arange(2 * BLK)[None, :]
    rel = BLK + qi - kj
    max_exact = NUM_BUCKETS // 2
    buckets, valid = [], []
    for window, dil in DILATED:
        w_sub = window // dil
        dist = np.maximum(rel, 0) * dil
        d_f = np.maximum(dist, max_exact).astype(np.float32)
        large = max_exact + (np.log(d_f / np.float32(max_exact)) / np.float32(math.log(MAX_DISTANCE / max_exact))
                             * np.float32(NUM_BUCKETS - max_exact)).astype(np.int32)
        large = np.minimum(large, NUM_BUCKETS - 1)
        buckets.append(np.where(dist < max_exact, dist, large).astype(np.int32))
        valid.append(((rel >= 0) & (rel <= w_sub)).astype(np.int32))
    return np.stack(buckets), np.stack(valid)


def _bias_kernel(rb_ref, bkt_ref, valid_ref, o_ref):
    h = pl.program_id(1)
    bkt = bkt_ref[...]
    acc = jnp.zeros(bkt.shape, F32)
    for j in range(NUM_BUCKETS):
        acc = jnp.where(bkt == j, rb_ref[j, h], acc)
    o_ref[...] = jnp.where(valid_ref[...] > 0, acc, NEG)


def _bias_tables(rel_bias):
    bkt, valid = _bucket_tables()
    npat = len(DILATED)
    blk = pl.BlockSpec((None, BLK, 2 * BLK), lambda p, h: (p, 0, 0))
    return pl.pallas_call(
        _bias_kernel,
        grid=(npat, ATTN_HEADS),
        in_specs=[pl.BlockSpec(memory_space=pltpu.SMEM), blk, blk],
        out_specs=pl.BlockSpec((None, None, BLK, 2 * BLK), lambda p, h: (p, h, 0, 0)),
        out_shape=jax.ShapeDtypeStruct((npat, ATTN_HEADS, BLK, 2 * BLK), F32),
        name="bias",
    )(rel_bias, jnp.asarray(bkt), jnp.asarray(valid))


def _in_proj_kernel(x_ref, ln1_ref, win_ref, sgnw_ref, sguw_ref, bst_ref, mixw_ref,
                    qkv_ref, sgu_ref, sgu_s):
    tm = x_ref.shape[0]
    h = _rms_rows(x_ref[...], ln1_ref[...]).astype(BF16)
    proj = jnp.dot(h, win_ref[...], preferred_element_type=F32)
    qkv_ref[...] = proj[:, :3 * D_ATTN]
    row = lax.broadcasted_iota(jnp.int32, (CHUNK, CHUNK), 0)
    col = lax.broadcasted_iota(jnp.int32, (CHUNK, CHUNK), 1)
    for g in range(SGU_GROUPS):
        u_lo = 3 * D_ATTN + g * SGU_CH
        v_lo = 3 * D_ATTN + D_SGU + g * SGU_CH
        ua = jax.nn.gelu(proj[:, u_lo:u_lo + SGU_CH])
        va = jax.nn.gelu(proj[:, v_lo:v_lo + SGU_CH])
        vn = _rms_rows(va, sgnw_ref[g:g + 1, :]).astype(BF16)
        w = jnp.where(row >= col, sguw_ref[g], 0.0).astype(BF16)
        bias = bst_ref[:, g:g + 1]
        for c in range(tm // CHUNK):
            rows = slice(c * CHUNK, (c + 1) * CHUNK)
            spatial = jnp.dot(w, vn[rows], preferred_element_type=F32) + bias
            sgu_s[rows, g * SGU_CH:(g + 1) * SGU_CH] = ua[rows] * spatial
    sgu_ref[...] = _rms_rows(sgu_s[...], mixw_ref[...]).astype(BF16)


def _in_proj(x2, ln1, w_in, sgu_norm_w, sgu_w, sgu_b, mixw_sgu, tm):
    t, d = x2.shape
    dproj = w_in.shape[1]
    full = lambda shape: pl.BlockSpec(shape, lambda i: (0,) * len(shape))
    return pl.pallas_call(
        _in_proj_kernel,
        grid=(t // tm,),
        in_specs=[pl.BlockSpec((tm, d), lambda i: (i, 0)),
                  full((1, d)), full((d, dproj)), full((SGU_GROUPS, SGU_CH)),
                  full((SGU_GROUPS, CHUNK, CHUNK)), full((CHUNK, SGU_GROUPS)), full((1, D_SGU))],
        out_specs=[pl.BlockSpec((tm, 3 * D_ATTN), lambda i: (i, 0)),
                   pl.BlockSpec((tm, D_SGU), lambda i: (i, 0))],
        out_shape=[jax.ShapeDtypeStruct((t, 3 * D_ATTN), F32),
                   jax.ShapeDtypeStruct((t, D_SGU), BF16)],
        scratch_shapes=[pltpu.VMEM((tm, D_SGU), F32)],
        compiler_params=pltpu.CompilerParams(dimension_semantics=("arbitrary",),
                                             vmem_limit_bytes=VMEM_LIMIT),
        name="in_proj",
    )(x2, ln1, w_in, sgu_norm_w, sgu_w, sgu_b.T, mixw_sgu)


def _attn_kernel(q_ref, k_ref, v_ref, qw_ref, kw_ref, bias_ref, o_ref,
                 qs, ks, vs, acc_s, m_s):
    s_len = q_ref.shape[0]
    heads = LANES // HEAD_DIM
    lane = lax.broadcasted_iota(jnp.int32, (s_len, LANES), 1)
    r_i = lax.broadcasted_iota(jnp.int32, (LANES, LANES), 0) // HEAD_DIM
    c_i = lax.broadcasted_iota(jnp.int32, (LANES, LANES), 1) // HEAD_DIM
    averager = jnp.where(r_i == c_i, 1.0 / HEAD_DIM, 0.0).astype(BF16)

    def head_rms(x, w):
        sq = x * x
        hi = sq.astype(BF16)
        lo = (sq - hi.astype(F32)).astype(BF16)
        mean = (jnp.dot(hi, averager, preferred_element_type=F32)
                + jnp.dot(lo, averager, preferred_element_type=F32))
        return x * lax.rsqrt(mean + EPS) * w

    qn = head_rms(q_ref[...], qw_ref[...]) * (HEAD_DIM ** -0.5)
    ks[...] = head_rms(k_ref[...], kw_ref[...])
    v = v_ref[...]
    for hh in range(heads):
        mine = (lane // HEAD_DIM) == hh
        qs[hh] = jnp.where(mine, qn, 0.0)
        vs[hh] = jnp.where(mine, v, 1.0)

    for p, (_, dil) in enumerate(DILATED):
        nblk = s_len // dil // BLK
        shift = dil.bit_length() - 1

        def group(qstarts, first, p=p, dil=dil):
            nk = BLK if first else 2 * BLK
            chains = [(hh, q0) for q0 in qstarts for hh in range(heads)]
            rows = lambda q0, n: pl.ds(q0, n, stride=dil)
            logits = []
            for hh, q0 in chains:
                qb = qs[hh, rows(q0, BLK), :].astype(BF16)
                kb = ks[rows(q0 - (nk - BLK) * dil, nk), :].astype(BF16)
                s = lax.dot_general(qb, kb, (((1,), (1,)), ((), ())), preferred_element_type=F32)
                bias = bias_ref[p, hh, :, BLK:] if first else bias_ref[p, hh]
                logits.append(jnp.where(bias > 0.1 * NEG, s + bias, NEG))
            probs = []
            for (hh, q0), lg in zip(chains, logits):
                m = jnp.max(lg, axis=-1, keepdims=True)
                m_s[hh, p, rows(q0, BLK), :] = jnp.broadcast_to(m, (BLK, LANES))
                probs.append(jnp.exp(lg - m).astype(BF16))
            for (hh, q0), pe in zip(chains, probs):
                vb = vs[hh, rows(q0 - (nk - BLK) * dil, nk), :].astype(BF16)
                acc_s[hh, p, rows(q0, BLK), :] = jnp.dot(pe, vb, preferred_element_type=F32)

        def first_groups(g, carry, group=group):
            group([g * ATTN_GROUP + j for j in range(min(dil, ATTN_GROUP))], True)
            return carry

        def later_groups(g, carry, group=group, dil=dil, shift=shift):
            starts = []
            for j in range(ATTN_GROUP):
                i = dil + g * ATTN_GROUP + j
                starts.append(jnp.bitwise_and(i, dil - 1) + jnp.right_shift(i, shift) * (dil * BLK))
            group(starts, False)
            return carry

        lax.fori_loop(0, -(-dil // ATTN_GROUP), first_groups, 0)
        n_later = (nblk - 1) * dil
        lax.fori_loop(0, n_later // ATTN_GROUP, later_groups, 0)
        tail = [dil + n_later // ATTN_GROUP * ATTN_GROUP + j for j in range(n_later % ATTN_GROUP)]
        if tail:
            group([(i & (dil - 1)) + (i >> shift) * (dil * BLK) for i in tail], False)

    out = None
    for hh in range(heads):
        ms = [m_s[hh, p] for p in range(len(DILATED))]
        m_all = functools.reduce(jnp.maximum, ms)
        tot = sum(jnp.exp(m - m_all) * acc_s[hh, p] for p, m in enumerate(ms))
        ratio = tot / pltpu.roll(tot, HEAD_DIM, axis=1)
        out = ratio if out is None else jnp.where((lane // HEAD_DIM) == hh, ratio, out)
    o_ref[...] = out


def _attention(qkv3, q_norm_w, k_norm_w, bias):
    b, s, _ = qkv3.shape
    nhp = D_ATTN // LANES
    heads = LANES // HEAD_DIM
    npat = len(DILATED)
    col = lambda off: pl.BlockSpec((None, s, LANES), lambda bi, hp, off=off: (bi, 0, off + hp))
    return pl.pallas_call(
        _attn_kernel,
        grid=(b, nhp),
        in_specs=[col(0), col(nhp), col(2 * nhp),
                  pl.BlockSpec((1, LANES), lambda bi, hp: (0, 0)),
                  pl.BlockSpec((1, LANES), lambda bi, hp: (0, 0)),
                  pl.BlockSpec((npat, LANES // HEAD_DIM, BLK, 2 * BLK), lambda bi, hp: (0, hp, 0, 0))],
        out_specs=pl.BlockSpec((None, s, LANES), lambda bi, hp: (bi, 0, hp)),
        out_shape=jax.ShapeDtypeStruct((b, s, D_ATTN), F32),
        scratch_shapes=[pltpu.VMEM((heads, s, LANES), F32), pltpu.VMEM((s, LANES), F32),
                        pltpu.VMEM((heads, s, LANES), F32),
                        pltpu.VMEM((heads, npat, s, LANES), F32),
                        pltpu.VMEM((heads, npat, s, LANES), F32)],
        compiler_params=pltpu.CompilerParams(dimension_semantics=("arbitrary", "arbitrary"),
                                             vmem_limit_bytes=VMEM_LIMIT),
        name="attn",
    )(qkv3, qkv3, qkv3, jnp.tile(q_norm_w, (1, heads)), jnp.tile(k_norm_w, (1, heads)), bias)


def _out_proj_kernel(attn_ref, sgu_ref, x_ref, mixw_ref, wout_ref, ln2_ref, x1_ref, hnt_ref):
    an = _rms_rows(attn_ref[...], mixw_ref[...]).astype(BF16)
    y = jnp.dot(an, wout_ref[:D_ATTN, :], preferred_element_type=F32)
    y = y + jnp.dot(sgu_ref[...], wout_ref[D_ATTN:, :], preferred_element_type=F32)
    x1 = x_ref[...] + y
    x1_ref[...] = x1
    hnt_ref[...] = _rms_rows(x1, ln2_ref[...]).T.astype(BF16)


def _out_proj(attn2, sgu_n, x2, mixw_attn, w_out, ln2, tm):
    t, d = x2.shape
    full = lambda shape: pl.BlockSpec(shape, lambda i: (0,) * len(shape))
    return pl.pallas_call(
        _out_proj_kernel,
        grid=(t // tm,),
        in_specs=[pl.BlockSpec((tm, D_ATTN), lambda i: (i, 0)),
                  pl.BlockSpec((tm, D_SGU), lambda i: (i, 0)),
                  pl.BlockSpec((tm, d), lambda i: (i, 0)),
                  full((1, D_ATTN)), full((D_ATTN + D_SGU, d)), full((1, d))],
        out_specs=[pl.BlockSpec((tm, d), lambda i: (i, 0)),
                   pl.BlockSpec((d, tm), lambda i: (0, i))],
        out_shape=[jax.ShapeDtypeStruct((t, d), F32), jax.ShapeDtypeStruct((d, t), BF16)],
        compiler_params=pltpu.CompilerParams(dimension_semantics=("arbitrary",),
                                             vmem_limit_bytes=VMEM_LIMIT),
        name="out_proj",
    )(attn2, sgu_n, x2, mixw_attn, w_out, ln2)


def _extract16(x, order, tie_break):
    rank = jnp.full(x.shape, PEER_TOPK, jnp.int32)
    vals = []
    cur = x
    for r in range(PEER_TOPK):
        m = jnp.max(cur, axis=0, keepdims=True)
        sel = cur == m
        if tie_break:
            first = jnp.min(jnp.where(sel, order, jnp.iinfo(jnp.int32).max), axis=0, keepdims=True)
            sel = order == first
        rank = jnp.where(sel, r, rank)
        cur = jnp.where(sel, -jnp.inf, cur)
        vals.append(m)
    return rank, vals


def _taken(rank):
    return jnp.sum(jnp.where(rank < PEER_TOPK, 1.0, 0.0))


def _staircase_counts(v1, v2, tb, tie_break):
    v2_all = jnp.concatenate(v2, axis=0)
    cands, poss, sizes = [], [], []
    for r1 in range(PEER_TOPK):
        rows = PEER_TOPK if r1 == 0 else 8
        r2 = lax.broadcasted_iota(jnp.int32, (rows, tb), 0)
        cands.append(jnp.where(r2 < PEER_TOPK // (r1 + 1), v1[r1] + v2_all[:rows], -jnp.inf))
        poss.append(r2 + r1 * PEER_TOPK)
        sizes.append(rows)
    cand = jnp.concatenate(cands, axis=0)
    pos = jnp.concatenate(poss, axis=0)
    rank, _ = _extract16(cand, pos, tie_break)
    picked = jnp.where(rank < PEER_TOPK, 1.0, 0.0)
    z = jnp.sum(picked * jnp.exp(cand - (v1[0] + v2[0])), axis=0, keepdims=True)
    counts, lo = [], 0
    for rows in sizes:
        counts.append(jnp.sum(picked[lo:lo + rows], axis=0, keepdims=True))
        lo += rows
    return counts, z, _taken(rank)


def _head_codes(s1, s2, tie_break):
    tb = s1.shape[1]
    iota = lax.broadcasted_iota(jnp.int32, s1.shape, 0)
    rank1, v1 = _extract16(s1, iota, tie_break)
    rank2, v2 = _extract16(s2, iota, tie_break)
    counts, z, taken = _staircase_counts(v1, v2, tb, tie_break)
    c1 = jnp.zeros(s1.shape, F32)
    for r1 in range(PEER_TOPK):
        c1 = jnp.where(rank1 == r1, counts[r1], c1)
    q1 = jnp.exp(s1 - v1[0]) / z
    c2 = rank2.astype(F32).astype(BF16)
    p2 = jnp.exp(s2 - v2[0]).astype(BF16)
    return (c1, q1, c2, p2), taken + _taken(rank1) + _taken(rank2)


def _route_kernel(hnt_ref, wqt_ref, k1_ref, k2_ref, c1_ref, q1_ref, c2_ref, p2_ref, qt_s):
    tb = hnt_ref.shape[1]
    half = PEER_D_KEY // 2
    qt_s[...] = jnp.dot(wqt_ref[...], hnt_ref[...], preferred_element_type=F32)

    def scores(h):
        base = pl.multiple_of(h * PEER_D_KEY, PEER_D_KEY)
        qa = qt_s[pl.ds(base, half), :].astype(BF16)
        qb = qt_s[pl.ds(base + half, half), :].astype(BF16)
        return (jnp.dot(k1_ref[h], qa, preferred_element_type=F32),
                jnp.dot(k2_ref[h], qb, preferred_element_type=F32))

    def heads(g, carry):
        hs = [g * ROUTE_HEADS + j for j in range(ROUTE_HEADS)]
        ss = [scores(h) for h in hs]
        fast = [_head_codes(s1, s2, tie_break=False) for s1, s2 in ss]
        for h, (s1, s2), (codes, taken) in zip(hs, ss, fast):
            untied = taken == float(3 * PEER_TOPK * tb)
            codes = lax.cond(untied, lambda codes=codes: codes,
                             lambda s1=s1, s2=s2: _head_codes(s1, s2, tie_break=True)[0])
            for ref, code in zip((c1_ref, q1_ref, c2_ref, p2_ref), codes):
                ref[h] = code
        return carry

    lax.fori_loop(0, PEER_HEADS // ROUTE_HEADS, heads, 0)


def _route(hnt, wqt, keys1, keys2, tb):
    d, t = hnt.shape
    nq = wqt.shape[0]
    full = lambda shape: pl.BlockSpec(shape, lambda i: (0,) * len(shape))
    code = pl.BlockSpec((PEER_HEADS, PEER_KEYS, tb), lambda i: (0, 0, i))
    code_shape = lambda dt: jax.ShapeDtypeStruct((PEER_HEADS, PEER_KEYS, t), dt)
    return pl.pallas_call(
        _route_kernel,
        grid=(t // tb,),
        in_specs=[pl.BlockSpec((d, tb), lambda i: (0, i)), full((nq, d)),
                  full(keys1.shape), full(keys2.shape)],
        out_specs=[code] * 4,
        out_shape=[code_shape(F32), code_shape(F32), code_shape(BF16), code_shape(BF16)],
        scratch_shapes=[pltpu.VMEM((nq, tb), F32)],
        compiler_params=pltpu.CompilerParams(dimension_semantics=("arbitrary",),
                                             vmem_limit_bytes=VMEM_LIMIT),
        name="route",
    )(hnt, wqt, keys1, keys2)


def _experts_kernel(hnt_ref, u_ref, vt_ref, c1_ref, q1_ref, c2_ref, p2_ref, x1_ref, o_ref,
                    acc_s, st_s, z_s, hnt_s, c2_s, p2_s, *, n_eb, n_blocks):
    t = pl.program_id(0)
    eb = u_ref.shape[0]
    tb = hnt_ref.shape[1]
    keys_per_block = eb // PEER_KEYS
    j_b = jnp.clip(t - 1, 0, n_blocks - 1) % n_eb
    j_c = jnp.clip(t - 2, 0, n_blocks - 1) % n_eb

    @pl.when(t == 0)
    def _():
        st_s[...] = jnp.zeros_like(st_s)
        z_s[...] = jnp.zeros_like(z_s)
        acc_s[...] = jnp.zeros_like(acc_s)

    @pl.when(jnp.logical_and(t >= 2, j_c == 0))
    def _():
        acc_s[...] = jnp.zeros_like(acc_s)

    @pl.when(jnp.minimum(t, n_blocks - 1) % n_eb == 0)
    def _():
        hnt_s[...] = hnt_ref[...]

    @pl.when(j_b == 0)
    def _():
        c2_s[...] = c2_ref[...]
        p2_s[...] = p2_ref[...]

    row0 = (j_b * keys_per_block) % CODE_ROW_BLOCK
    rows_of = lambda ref, ii: [ref[h, pl.ds(row0 + ii, 1), :] for h in range(PEER_HEADS)]
    c1_all = [rows_of(c1_ref, ii) for ii in range(keys_per_block)]
    q1_all = [rows_of(q1_ref, ii) for ii in range(keys_per_block)]
    packed = (PEER_KEYS // BF16_ROWS, BF16_ROWS, LANES)
    row_bf16 = lambda r: jnp.broadcast_to(r, (BF16_ROWS, LANES)).astype(BF16)[None]

    def stage_a(cols):
        st_s[:, cols] = jnp.dot(u_ref[...], hnt_s[:, cols], preferred_element_type=F32)

    def stage_b(iis, cols):
        gates = [None] * len(iis)
        for h in range(PEER_HEADS):
            c2 = c2_s[h, :, cols].reshape(packed)
            p2 = p2_s[h, :, cols].reshape(packed)
            for n, ii in enumerate(iis):
                c1 = row_bf16(c1_all[ii][h][:, cols])
                q1 = row_bf16(q1_all[ii][h][:, cols])
                term = jnp.where(c2 < c1, p2 * q1, jnp.zeros((), BF16))
                gates[n] = term if gates[n] is None else gates[n] + term
        for n, ii in enumerate(iis):
            rows = slice(ii * PEER_KEYS, (ii + 1) * PEER_KEYS)
            act = jax.nn.gelu(st_s[rows, cols]).astype(BF16).reshape(packed)
            z_s[rows, cols] = (act * gates[n]).reshape(PEER_KEYS, LANES)

    def stage_c(cols):
        acc_s[:, cols] += jnp.dot(vt_ref[...], z_s[:, cols], preferred_element_type=F32)

    for mc in range(tb // MXU_COLS):
        stage_c(slice(mc * MXU_COLS, (mc + 1) * MXU_COLS))
        for lc in range(MXU_COLS // LANES):
            lo = mc * MXU_COLS + lc * LANES
            for ii in range(0, keys_per_block, GATE_ROWS_PER_PASS):
                stage_b(list(range(ii, ii + GATE_ROWS_PER_PASS)), slice(lo, lo + LANES))
        stage_a(slice(mc * MXU_COLS, (mc + 1) * MXU_COLS))

    @pl.when(jnp.logical_and(t >= 2, j_c == n_eb - 1))
    def _():
        o_ref[...] = x1_ref[...] + acc_s[...].T


def _experts(hnt, u_tab, vt_blocks, codes, x1, tb):
    d, t = hnt.shape
    n_eb, _, eb = vt_blocks.shape
    keys_per_block = eb // PEER_KEYS
    assert CODE_ROW_BLOCK % keys_per_block == 0
    n_blocks = (t // tb) * n_eb
    g_a = lambda s: jnp.minimum(s, n_blocks - 1)
    g_b = lambda s: jnp.clip(s - 1, 0, n_blocks - 1)
    g_c = lambda s: jnp.clip(s - 2, 0, n_blocks - 1)
    code_rows = pl.BlockSpec(
        (PEER_HEADS, CODE_ROW_BLOCK, tb),
        lambda s: (0, (g_b(s) % n_eb) * keys_per_block // CODE_ROW_BLOCK, g_b(s) // n_eb))
    code_keys = pl.BlockSpec((PEER_HEADS, PEER_KEYS, tb), lambda s: (0, 0, g_b(s) // n_eb))
    assert codes[0].dtype == F32 and codes[1].dtype == F32
    assert codes[2].dtype == BF16 and codes[3].dtype == BF16
    return pl.pallas_call(
        functools.partial(_experts_kernel, n_eb=n_eb, n_blocks=n_blocks),
        grid=(n_blocks + 2,),
        in_specs=[pl.BlockSpec((d, tb), lambda s: (0, g_a(s) // n_eb)),
                  pl.BlockSpec((eb, d), lambda s: (g_a(s) % n_eb, 0)),
                  pl.BlockSpec((None, d, eb), lambda s: (g_c(s) % n_eb, 0, 0)),
                  code_rows, code_rows, code_keys, code_keys,
                  pl.BlockSpec((tb, d), lambda s: (g_c(s) // n_eb, 0))],
        out_specs=pl.BlockSpec((tb, d), lambda s: (g_c(s) // n_eb, 0)),
        out_shape=jax.ShapeDtypeStruct((t, d), F32),
        scratch_shapes=[pltpu.VMEM((d, tb), F32), pltpu.VMEM((eb, tb), F32),
                        pltpu.VMEM((eb, tb), BF16), pltpu.VMEM((d, tb), BF16),
                        pltpu.VMEM((PEER_HEADS, PEER_KEYS, tb), BF16),
                        pltpu.VMEM((PEER_HEADS, PEER_KEYS, tb), BF16)],
        compiler_params=pltpu.CompilerParams(dimension_semantics=("arbitrary",),
                                             vmem_limit_bytes=VMEM_LIMIT),
        name="experts",
    )(hnt, u_tab, vt_blocks, *codes, x1)


def kernel(x, ln1_w, w_in, q_norm_w, k_norm_w, sgu_norm_w, sgu_w, sgu_b, mix_norm_w, w_out,
           ln2_w, peer_w_query, peer_keys1, peer_keys2, peer_u, peer_v, rel_bias):
    b, s, d = x.shape
    t = b * s
    bias = _bias_tables(rel_bias)
    for l in range(ln1_w.shape[0]):
        x2 = x.reshape(t, d)
        qkv, sgu_n = _in_proj(x2, ln1_w[l][None], w_in[l].astype(BF16), sgu_norm_w[l], sgu_w[l],
                              sgu_b[l], mix_norm_w[l, D_ATTN:][None], tm=512)
        attn = _attention(qkv.reshape(b, s, 3 * D_ATTN), q_norm_w[l][None], k_norm_w[l][None], bias)
        x1, hnt = _out_proj(attn.reshape(t, D_ATTN), sgu_n, x2, mix_norm_w[l, :D_ATTN][None],
                            w_out[l].astype(BF16), ln2_w[l][None], tm=512)
        codes = _route(hnt, peer_w_query[l].T.astype(BF16), peer_keys1[l].astype(BF16),
                       peer_keys2[l].astype(BF16), tb=256)
        vt_blocks = peer_v[l].reshape(-1, EXPERT_BLOCK, d).transpose(0, 2, 1).astype(BF16)
        out = _experts(hnt, peer_u[l].astype(BF16), vt_blocks, codes, x1, tb=EXPERT_TOKENS)
        x = out.reshape(b, s, d)
    return x
```

```python
import functools
import math

import numpy as np
import jax
import jax.numpy as jnp
from jax import lax
from jax.experimental import pallas as pl
from jax.experimental.pallas import tpu as pltpu

F32 = jnp.float32
BF16 = jnp.bfloat16

ATTN_HEADS = 8
HEAD_DIM = 64
D_ATTN = ATTN_HEADS * HEAD_DIM
SGU_GROUPS = 4
SGU_CH = 128
D_SGU = SGU_GROUPS * SGU_CH
CHUNK = 128
DILATED = ((128, 1), (512, 4), (2048, 16))
BLK = 128
NUM_BUCKETS = 32
MAX_DISTANCE = 2048
PEER_HEADS = 8
PEER_KEYS = 128
PEER_D_KEY = 256
PEER_TOPK = 16
EPS = 1e-6
NEG = -1e30

LANES = 128
BF16_ROWS = 16
MXU_COLS = 256
CODE_ROW_BLOCK = 8
ATTN_GROUP = 4
ROUTE_HEADS = 2
GATE_ROWS_PER_PASS = 2
EXPERT_BLOCK = 1024
EXPERT_TOKENS = 512
VMEM_LIMIT = 56 * 1024 * 1024


def _rms_rows(x, w):
    return x * lax.rsqrt(jnp.mean(x * x, axis=-1, keepdims=True) + EPS) * w


def _bucket_tables():
    qi = np.arange(BLK)[:, None]
    kj = np.arange(2 * BLK)[None, :]
    rel = BLK + qi - kj
    max_exact = NUM_BUCKETS // 2
    buckets, valid = [], []
    for window, dil in DILATED:
        w_sub = window // dil
        dist = np.maximum(rel, 0) * dil
        d_f = np.maximum(dist, max_exact).astype(np.float32)
        large = max_exact + (np.log(d_f / np.float32(max_exact)) / np.float32(math.log(MAX_DISTANCE / max_exact))
                             * np.float32(NUM_BUCKETS - max_exact)).astype(np.int32)
        large = np.minimum(large, NUM_BUCKETS - 1)
        buckets.append(np.where(dist < max_exact, dist, large).astype(np.int32))
        valid.append(((rel >= 0) & (rel <= w_sub)).astype(np.int32))
    return np.stack(buckets), np.stack(valid)


def _bias_kernel(rb_ref, bkt_ref, valid_ref, o_ref):
    h = pl.program_id(1)
    bkt = bkt_ref[...]
    acc = jnp.zeros(bkt.shape, F32)
    for j in range(NUM_BUCKETS):
        acc = jnp.where(bkt == j, rb_ref[j, h], acc)
    o_ref[...] = jnp.where(valid_ref[...] > 0, acc, NEG)


def _bias_tables(rel_bias):
    bkt, valid = _bucket_tables()
    npat = len(DILATED)
    blk = pl.BlockSpec((None, BLK, 2 * BLK), lambda p, h: (p, 0, 0))
    return pl.pallas_call(
        _bias_kernel,
        grid=(npat, ATTN_HEADS),
        in_specs=[pl.BlockSpec(memory_space=pltpu.SMEM), blk, blk],
        out_specs=pl.BlockSpec((None, None, BLK, 2 * BLK), lambda p, h: (p, h, 0, 0)),
        out_shape=jax.ShapeDtypeStruct((npat, ATTN_HEADS, BLK, 2 * BLK), F32),
        name="bias",
    )(rel_bias, jnp.asarray(bkt), jnp.asarray(valid))


def _in_proj_kernel(x_ref, ln1_ref, win_ref, sgnw_ref, sguw_ref, bst_ref, mixw_ref,
                    qkv_ref, sgu_ref, sgu_s):
    tm = x_ref.shape[0]
    h = _rms_rows(x_ref[...], ln1_ref[...]).astype(BF16)
    proj = jnp.dot(h, win_ref[...], preferred_element_type=F32)
    qkv_ref[...] = proj[:, :3 * D_ATTN]
    row = lax.broadcasted_iota(jnp.int32, (CHUNK, CHUNK), 0)
    col = lax.broadcasted_iota(jnp.int32, (CHUNK, CHUNK), 1)
    for g in range(SGU_GROUPS):
        u_lo = 3 * D_ATTN + g * SGU_CH
        v_lo = 3 * D_ATTN + D_SGU + g * SGU_CH
        ua = jax.nn.gelu(proj[:, u_lo:u_lo + SGU_CH])
        va = jax.nn.gelu(proj[:, v_lo:v_lo + SGU_CH])
        vn = _rms_rows(va, sgnw_ref[g:g + 1, :]).astype(BF16)
        w = jnp.where(row >= col, sguw_ref[g], 0.0).astype(BF16)
        bias = bst_ref[:, g:g + 1]
        for c in range(tm // CHUNK):
            rows = slice(c * CHUNK, (c + 1) * CHUNK)
            spatial = jnp.dot(w, vn[rows], preferred_element_type=F32) + bias
            sgu_s[rows, g * SGU_CH:(g + 1) * SGU_CH] = ua[rows] * spatial
    sgu_ref[...] = _rms_rows(sgu_s[...], mixw_ref[...]).astype(BF16)


def _in_proj(x2, ln1, w_in, sgu_norm_w, sgu_w, sgu_b, mixw_sgu, tm):
    t, d = x2.shape
    dproj = w_in.shape[1]
    full = lambda shape: pl.BlockSpec(shape, lambda i: (0,) * len(shape))
    return pl.pallas_call(
        _in_proj_kernel,
        grid=(t // tm,),
        in_specs=[pl.BlockSpec((tm, d), lambda i: (i, 0)),
                  full((1, d)), full((d, dproj)), full((SGU_GROUPS, SGU_CH)),
                  full((SGU_GROUPS, CHUNK, CHUNK)), full((CHUNK, SGU_GROUPS)), full((1, D_SGU))],
        out_specs=[pl.BlockSpec((tm, 3 * D_ATTN), lambda i: (i, 0)),
                   pl.BlockSpec((tm, D_SGU), lambda i: (i, 0))],
        out_shape=[jax.ShapeDtypeStruct((t, 3 * D_ATTN), F32),
                   jax.ShapeDtypeStruct((t, D_SGU), BF16)],
        scratch_shapes=[pltpu.VMEM((tm, D_SGU), F32)],
        compiler_params=pltpu.CompilerParams(dimension_semantics=("arbitrary",),
                                             vmem_limit_bytes=VMEM_LIMIT),
        name="in_proj",
    )(x2, ln1, w_in, sgu_norm_w, sgu_w, sgu_b.T, mixw_sgu)


def _attn_kernel(q_ref, k_ref, v_ref, qw_ref, kw_ref, bias_ref, o_ref,
                 qs, ks, vs, acc_s, m_s):
    s_len = q_ref.shape[0]
    heads = LANES // HEAD_DIM
    lane = lax.broadcasted_iota(jnp.int32, (s_len, LANES), 1)
    r_i = lax.broadcasted_iota(jnp.int32, (LANES, LANES), 0) // HEAD_DIM
    c_i = lax.broadcasted_iota(jnp.int32, (LANES, LANES), 1) // HEAD_DIM
    averager = jnp.where(r_i == c_i, 1.0 / HEAD_DIM, 0.0).astype(BF16)

    def head_rms(x, w):
        sq = x * x
        hi = sq.astype(BF16)
        lo = (sq - hi.astype(F32)).astype(BF16)
        mean = (jnp.dot(hi, averager, preferred_element_type=F32)
                + jnp.dot(lo, averager, preferred_element_type=F32))
        return x * lax.rsqrt(mean + EPS) * w

    qn = head_rms(q_ref[...], qw_ref[...]) * (HEAD_DIM ** -0.5)
    ks[...] = head_rms(k_ref[...], kw_ref[...])
    v = v_ref[...]
    for hh in range(heads):
        mine = (lane // HEAD_DIM) == hh
        qs[hh] = jnp.where(mine, qn, 0.0)
        vs[hh] = jnp.where(mine, v, 1.0)

    for p, (_, dil) in enumerate(DILATED):
        nblk = s_len // dil // BLK
        shift = dil.bit_length() - 1

        def group(qstarts, first, p=p, dil=dil):
            nk = BLK if first else 2 * BLK
            chains = [(hh, q0) for q0 in qstarts for hh in range(heads)]
            rows = lambda q0, n: pl.ds(q0, n, stride=dil)
            logits = []
            for hh, q0 in chains:
                qb = qs[hh, rows(q0, BLK), :].astype(BF16)
                kb = ks[rows(q0 - (nk - BLK) * dil, nk), :].astype(BF16)
                s = lax.dot_general(qb, kb, (((1,), (1,)), ((), ())), preferred_element_type=F32)
                bias = bias_ref[p, hh, :, BLK:] if first else bias_ref[p, hh]
                logits.append(jnp.where(bias > 0.1 * NEG, s + bias, NEG))
            probs = []
            for (hh, q0), lg in zip(chains, logits):
                m = jnp.max(lg, axis=-1, keepdims=True)
                m_s[hh, p, rows(q0, BLK), :] = jnp.broadcast_to(m, (BLK, LANES))
                probs.append(jnp.exp(lg - m).astype(BF16))
            for (hh, q0), pe in zip(chains, probs):
                vb = vs[hh, rows(q0 - (nk - BLK) * dil, nk), :].astype(BF16)
                acc_s[hh, p, rows(q0, BLK), :] = jnp.dot(pe, vb, preferred_element_type=F32)

        def first_groups(g, carry, group=group):
            group([g * ATTN_GROUP + j for j in range(min(dil, ATTN_GROUP))], True)
            return carry

        def later_groups(g, carry, group=group, dil=dil, shift=shift):
            starts = []
            for j in range(ATTN_GROUP):
                i = dil + g * ATTN_GROUP + j
                starts.append(jnp.bitwise_and(i, dil - 1) + jnp.right_shift(i, shift) * (dil * BLK))
            group(starts, False)
            return carry

        lax.fori_loop(0, -(-dil // ATTN_GROUP), first_groups, 0)
        n_later = (nblk - 1) * dil
        lax.fori_loop(0, n_later // ATTN_GROUP, later_groups, 0)
        tail = [dil + n_later // ATTN_GROUP * ATTN_GROUP + j for j in range(n_later % ATTN_GROUP)]
        if tail:
            group([(i & (dil - 1)) + (i >> shift) * (dil * BLK) for i in tail], False)

    out = None
    for hh in range(heads):
        ms = [m_s[hh, p] for p in range(len(DILATED))]
        m_all = functools.reduce(jnp.maximum, ms)
        tot = sum(jnp.exp(m - m_all) * acc_s[hh, p] for p, m in enumerate(ms))
        ratio = tot / pltpu.roll(tot, HEAD_DIM, axis=1)
        out = ratio if out is None else jnp.where((lane // HEAD_DIM) == hh, ratio, out)
    o_ref[...] = out


def _attention(qkv3, q_norm_w, k_norm_w, bias):
    b, s, _ = qkv3.shape
    nhp = D_ATTN // LANES
    heads = LANES // HEAD_DIM
    npat = len(DILATED)
    col = lambda off: pl.BlockSpec((None, s, LANES), lambda bi, hp, off=off: (bi, 0, off + hp))
    return pl.pallas_call(
        _attn_kernel,
        grid=(b, nhp),
        in_specs=[col(0), col(nhp), col(2 * nhp),
                  pl.BlockSpec((1, LANES), lambda bi, hp: (0, 0)),
                  pl.BlockSpec((1, LANES), lambda bi, hp: (0, 0)),
                  pl.BlockSpec((npat, LANES // HEAD_DIM, BLK, 2 * BLK), lambda bi, hp: (0, hp, 0, 0))],
        out_specs=pl.BlockSpec((None, s, LANES), lambda bi, hp: (bi, 0, hp)),
        out_shape=jax.ShapeDtypeStruct((b, s, D_ATTN), F32),
        scratch_shapes=[pltpu.VMEM((heads, s, LANES), F32), pltpu.VMEM((s, LANES), F32),
                        pltpu.VMEM((heads, s, LANES), F32),
                        pltpu.VMEM((heads, npat, s, LANES), F32),
                        pltpu.VMEM((heads, npat, s, LANES), F32)],
        compiler_params=pltpu.CompilerParams(dimension_semantics=("arbitrary", "arbitrary"),
                                             vmem_limit_bytes=VMEM_LIMIT),
        name="attn",
    )(qkv3, qkv3, qkv3, jnp.tile(q_norm_w, (1, heads)), jnp.tile(k_norm_w, (1, heads)), bias)


def _out_proj_kernel(attn_ref, sgu_ref, x_ref, mixw_ref, wout_ref, ln2_ref, x1_ref, hnt_ref):
    an = _rms_rows(attn_ref[...], mixw_ref[...]).astype(BF16)
    y = jnp.dot(an, wout_ref[:D_ATTN, :], preferred_element_type=F32)
    y = y + jnp.dot(sgu_ref[...], wout_ref[D_ATTN:, :], preferred_element_type=F32)
    x1 = x_ref[...] + y
    x1_ref[...] = x1
    hnt_ref[...] = _rms_rows(x1, ln2_ref[...]).T.astype(BF16)


def _out_proj(attn2, sgu_n, x2, mixw_attn, w_out, ln2, tm):
    t, d = x2.shape
    full = lambda shape: pl.BlockSpec(shape, lambda i: (0,) * len(shape))
    return pl.pallas_call(
        _out_proj_kernel,
        grid=(t // tm,),
        in_specs=[pl.BlockSpec((tm, D_ATTN), lambda i: (i, 0)),
                  pl.BlockSpec((tm, D_SGU), lambda i: (i, 0)),
                  pl.BlockSpec((tm, d), lambda i: (i, 0)),
                  full((1, D_ATTN)), full((D_ATTN + D_SGU, d)), full((1, d))],
        out_specs=[pl.BlockSpec((tm, d), lambda i: (i, 0)),
                   pl.BlockSpec((d, tm), lambda i: (0, i))],
        out_shape=[jax.ShapeDtypeStruct((t, d), F32), jax.ShapeDtypeStruct((d, t), BF16)],
        compiler_params=pltpu.CompilerParams(dimension_semantics=("arbitrary",),
                                             vmem_limit_bytes=VMEM_LIMIT),
        name="out_proj",
    )(attn2, sgu_n, x2, mixw_attn, w_out, ln2)


def _extract16(x, order, tie_break):
    rank = jnp.full(x.shape, PEER_TOPK, jnp.int32)
    vals = []
    cur = x
    for r in range(PEER_TOPK):
        m = jnp.max(cur, axis=0, keepdims=True)
        sel = cur == m
        if tie_break:
            first = jnp.min(jnp.where(sel, order, jnp.iinfo(jnp.int32).max), axis=0, keepdims=True)
            sel = order == first
        rank = jnp.where(sel, r, rank)
        cur = jnp.where(sel, -jnp.inf, cur)
        vals.append(m)
    return rank, vals


def _taken(rank):
    return jnp.sum(jnp.where(rank < PEER_TOPK, 1.0, 0.0))


def _staircase_counts(v1, v2, tb, tie_break):
    v2_all = jnp.concatenate(v2, axis=0)
    cands, poss, sizes = [], [], []
    for r1 in range(PEER_TOPK):
        rows = PEER_TOPK if r1 == 0 else 8
        r2 = lax.broadcasted_iota(jnp.int32, (rows, tb), 0)
        cands.append(jnp.where(r2 < PEER_TOPK // (r1 + 1), v1[r1] + v2_all[:rows], -jnp.inf))
        poss.append(r2 + r1 * PEER_TOPK)
        sizes.append(rows)
    cand = jnp.concatenate(cands, axis=0)
    pos = jnp.concatenate(poss, axis=0)
    rank, _ = _extract16(cand, pos, tie_break)
    picked = jnp.where(rank < PEER_TOPK, 1.0, 0.0)
    z = jnp.sum(picked * jnp.exp(cand - (v1[0] + v2[0])), axis=0, keepdims=True)
    counts, lo = [], 0
    for rows in sizes:
        counts.append(jnp.sum(picked[lo:lo + rows], axis=0, keepdims=True))
        lo += rows
    return counts, z, _taken(rank)


def _head_codes(s1, s2, tie_break):
    tb = s1.shape[1]
    iota = lax.broadcasted_iota(jnp.int32, s1.shape, 0)
    rank1, v1 = _extract16(s1, iota, tie_break)
    rank2, v2 = _extract16(s2, iota, tie_break)
    counts, z, taken = _staircase_counts(v1, v2, tb, tie_break)
    c1 = jnp.zeros(s1.shape, F32)
    for r1 in range(PEER_TOPK):
        c1 = jnp.where(rank1 == r1, counts[r1], c1)
    q1 = jnp.exp(s1 - v1[0]) / z
    c2 = rank2.astype(F32).astype(BF16)
    p2 = jnp.exp(s2 - v2[0]).astype(BF16)
    return (c1, q1, c2, p2), taken + _taken(rank1) + _taken(rank2)


def _route_kernel(hnt_ref, wqt_ref, k1_ref, k2_ref, c1_ref, q1_ref, c2_ref, p2_ref, qt_s):
    tb = hnt_ref.shape[1]
    half = PEER_D_KEY // 2
    qt_s[...] = jnp.dot(wqt_ref[...], hnt_ref[...], preferred_element_type=F32)

    def scores(h):
        base = pl.multiple_of(h * PEER_D_KEY, PEER_D_KEY)
        qa = qt_s[pl.ds(base, half), :].astype(BF16)
        qb = qt_s[pl.ds(base + half, half), :].astype(BF16)
        return (jnp.dot(k1_ref[h], qa, preferred_element_type=F32),
                jnp.dot(k2_ref[h], qb, preferred_element_type=F32))

    def heads(g, carry):
        hs = [g * ROUTE_HEADS + j for j in range(ROUTE_HEADS)]
        ss = [scores(h) for h in hs]
        fast = [_head_codes(s1, s2, tie_break=False) for s1, s2 in ss]
        for h, (s1, s2), (codes, taken) in zip(hs, ss, fast):
            untied = taken == float(3 * PEER_TOPK * tb)
            codes = lax.cond(untied, lambda codes=codes: codes,
                             lambda s1=s1, s2=s2: _head_codes(s1, s2, tie_break=True)[0])
            for ref, code in zip((c1_ref, q1_ref, c2_ref, p2_ref), codes):
                ref[h] = code
        return carry

    lax.fori_loop(0, PEER_HEADS // ROUTE_HEADS, heads, 0)


def _route(hnt, wqt, keys1, keys2, tb):
    d, t = hnt.shape
    nq = wqt.shape[0]
    full = lambda shape: pl.BlockSpec(shape, lambda i: (0,) * len(shape))
    code = pl.BlockSpec((PEER_HEADS, PEER_KEYS, tb), lambda i: (0, 0, i))
    code_shape = lambda dt: jax.ShapeDtypeStruct((PEER_HEADS, PEER_KEYS, t), dt)
    return pl.pallas_call(
        _route_kernel,
        grid=(t // tb,),
        in_specs=[pl.BlockSpec((d, tb), lambda i: (0, i)), full((nq, d)),
                  full(keys1.shape), full(keys2.shape)],
        out_specs=[code] * 4,
        out_shape=[code_shape(F32), code_shape(F32), code_shape(BF16), code_shape(BF16)],
        scratch_shapes=[pltpu.VMEM((nq, tb), F32)],
        compiler_params=pltpu.CompilerParams(dimension_semantics=("arbitrary",),
                                             vmem_limit_bytes=VMEM_LIMIT),
        name="route",
    )(hnt, wqt, keys1, keys2)


def _experts_kernel(hnt_ref, u_ref, vt_ref, c1_ref, q1_ref, c2_ref, p2_ref, x1_ref, o_ref,
                    acc_s, st_s, z_s, hnt_s, c2_s, p2_s, *, n_eb):
    j = pl.program_id(0) % n_eb
    eb = u_ref.shape[0]
    tb = hnt_ref.shape[1]
    keys_per_block = eb // PEER_KEYS

    @pl.when(j == 0)
    def _():
        acc_s[...] = jnp.zeros_like(acc_s)
        hnt_s[...] = hnt_ref[...]
        c2_s[...] = c2_ref[...]
        p2_s[...] = p2_ref[...]

    row0 = (j * keys_per_block) % CODE_ROW_BLOCK
    rows_of = lambda ref, ii: [ref[h, pl.ds(row0 + ii, 1), :] for h in range(PEER_HEADS)]
    c1_all = [rows_of(c1_ref, ii) for ii in range(keys_per_block)]
    q1_all = [rows_of(q1_ref, ii) for ii in range(keys_per_block)]
    packed = (PEER_KEYS // BF16_ROWS, BF16_ROWS, LANES)
    row_bf16 = lambda r: jnp.broadcast_to(r, (BF16_ROWS, LANES)).astype(BF16)[None]

    def stage_a(cols):
        st_s[:, cols] = jnp.dot(u_ref[...], hnt_s[:, cols], preferred_element_type=F32)

    def stage_b(iis, cols):
        gates = [None] * len(iis)
        for h in range(PEER_HEADS):
            c2 = c2_s[h, :, cols].reshape(packed)
            p2 = p2_s[h, :, cols].reshape(packed)
            for n, ii in enumerate(iis):
                c1 = row_bf16(c1_all[ii][h][:, cols])
                q1 = row_bf16(q1_all[ii][h][:, cols])
                term = jnp.where(c2 < c1, p2 * q1, jnp.zeros((), BF16))
                gates[n] = term if gates[n] is None else gates[n] + term
        for n, ii in enumerate(iis):
            rows = slice(ii * PEER_KEYS, (ii + 1) * PEER_KEYS)
            act = jax.nn.gelu(st_s[rows, cols]).astype(BF16).reshape(packed)
            z_s[rows, cols] = (act * gates[n]).reshape(PEER_KEYS, LANES)

    def stage_c(cols):
        acc_s[:, cols] += jnp.dot(vt_ref[...], z_s[:, cols], preferred_element_type=F32)

    groups = [slice(mc * MXU_COLS, (mc + 1) * MXU_COLS) for mc in range(tb // MXU_COLS)]
    for cols in groups:
        stage_a(cols)
    for cols in groups:
        for lo in range(cols.start, cols.stop, LANES):
            for ii in range(0, keys_per_block, GATE_ROWS_PER_PASS):
                stage_b(list(range(ii, ii + GATE_ROWS_PER_PASS)), slice(lo, lo + LANES))
        stage_c(cols)

    @pl.when(j == n_eb - 1)
    def _():
        o_ref[...] = x1_ref[...] + acc_s[...].T


def _experts(hnt, u_tab, vt_blocks, codes, x1, tb):
    d, t = hnt.shape
    n_eb, _, eb = vt_blocks.shape
    keys_per_block = eb // PEER_KEYS
    assert CODE_ROW_BLOCK % keys_per_block == 0
    code_rows = pl.BlockSpec((PEER_HEADS, CODE_ROW_BLOCK, tb),
                             lambda s: (0, (s % n_eb) * keys_per_block // CODE_ROW_BLOCK, s // n_eb))
    code_keys = pl.BlockSpec((PEER_HEADS, PEER_KEYS, tb), lambda s: (0, 0, s // n_eb))
    assert codes[0].dtype == F32 and codes[1].dtype == F32
    assert codes[2].dtype == BF16 and codes[3].dtype == BF16
    return pl.pallas_call(
        functools.partial(_experts_kernel, n_eb=n_eb),
        grid=((t // tb) * n_eb,),
        in_specs=[pl.BlockSpec((d, tb), lambda s: (0, s // n_eb)),
                  pl.BlockSpec((eb, d), lambda s: (s % n_eb, 0)),
                  pl.BlockSpec((None, d, eb), lambda s: (s % n_eb, 0, 0)),
                  code_rows, code_rows, code_keys, code_keys,
                  pl.BlockSpec((tb, d), lambda s: (s // n_eb, 0))],
        out_specs=pl.BlockSpec((tb, d), lambda s: (s // n_eb, 0)),
        out_shape=jax.ShapeDtypeStruct((t, d), F32),
        scratch_shapes=[pltpu.VMEM((d, tb), F32), pltpu.VMEM((eb, tb), F32),
                        pltpu.VMEM((eb, tb), BF16), pltpu.VMEM((d, tb), BF16),
                        pltpu.VMEM((PEER_HEADS, PEER_KEYS, tb), BF16),
                        pltpu.VMEM((PEER_HEADS, PEER_KEYS, tb), BF16)],
        compiler_params=pltpu.CompilerParams(dimension_semantics=("arbitrary",),
                                             vmem_limit_bytes=VMEM_LIMIT),
        name="experts",
    )(hnt, u_tab, vt_blocks, *codes, x1)


def kernel(x, ln1_w, w_in, q_norm_w, k_norm_w, sgu_norm_w, sgu_w, sgu_b, mix_norm_w, w_out,
           ln2_w, peer_w_query, peer_keys1, peer_keys2, peer_u, peer_v, rel_bias):
    b, s, d = x.shape
    t = b * s
    bias = _bias_tables(rel_bias)
    for l in range(ln1_w.shape[0]):
        x2 = x.reshape(t, d)
        qkv, sgu_n = _in_proj(x2, ln1_w[l][None], w_in[l].astype(BF16), sgu_norm_w[l], sgu_w[l],
                              sgu_b[l], mix_norm_w[l, D_ATTN:][None], tm=512)
        attn = _attention(qkv.reshape(b, s, 3 * D_ATTN), q_norm_w[l][None], k_norm_w[l][None], bias)
        x1, hnt = _out_proj(attn.reshape(t, D_ATTN), sgu_n, x2, mix_norm_w[l, :D_ATTN][None],
                            w_out[l].astype(BF16), ln2_w[l][None], tm=512)
        codes = _route(hnt, peer_w_query[l].T.astype(BF16), peer_keys1[l].astype(BF16),
                       peer_keys2[l].astype(BF16), tb=256)
        vt_blocks = peer_v[l].reshape(-1, EXPERT_BLOCK, d).transpose(0, 2, 1).astype(BF16)
        out = _experts(hnt, peer_u[l].astype(BF16), vt_blocks, codes, x1, tb=EXPERT_TOKENS)
        x = out.reshape(b, s, d)
    return x
```

```python
import functools
import math

import numpy as np
import jax
import jax.numpy as jnp
from jax import lax
from jax.experimental import pallas as pl
from jax.experimental.pallas import tpu as pltpu

F32 = jnp.float32
BF16 = jnp.bfloat16

ATTN_HEADS = 8
HEAD_DIM = 64
D_ATTN = ATTN_HEADS * HEAD_DIM
SGU_GROUPS = 4
SGU_CH = 128
D_SGU = SGU_GROUPS * SGU_CH
CHUNK = 128
DILATED = ((128, 1), (512, 4), (2048, 16))
BLK = 128
NUM_BUCKETS = 32
MAX_DISTANCE = 2048
PEER_HEADS = 8
PEER_KEYS = 128
PEER_D_KEY = 256
PEER_TOPK = 16
EPS = 1e-6
NEG = -1e30

LANES = 128
ATTN_GROUP = 4
ROUTE_HEADS = 2
EXPERT_BLOCK = 512
EXPERT_TOKENS = 512
GATE_TILE = 2 * LANES
GATE_ROWS = 64
VMEM_LIMIT = 56 * 1024 * 1024


def _rms_rows(x, w):
    return x * lax.rsqrt(jnp.mean(x * x, axis=-1, keepdims=True) + EPS) * w


def _bucket_tables():
    qi = np.arange(BLK)[:, None]
    kj = np.arange(2 * BLK)[None, :]
    rel = BLK + qi - kj
    max_exact = NUM_BUCKETS // 2
    buckets, valid = [], []
    for window, dil in DILATED:
        w_sub = window // dil
        dist = np.maximum(rel, 0) * dil
        d_f = np.maximum(dist, max_exact).astype(np.float32)
        large = max_exact + (np.log(d_f / np.float32(max_exact)) / np.float32(math.log(MAX_DISTANCE / max_exact))
                             * np.float32(NUM_BUCKETS - max_exact)).astype(np.int32)
        large = np.minimum(large, NUM_BUCKETS - 1)
        buckets.append(np.where(dist < max_exact, dist, large).astype(np.int32))
        valid.append(((rel >= 0) & (rel <= w_sub)).astype(np.int32))
    return np.stack(buckets), np.stack(valid)


def _bias_kernel(rb_ref, bkt_ref, valid_ref, o_ref):
    h = pl.program_id(1)
    bkt = bkt_ref[...]
    acc = jnp.zeros(bkt.shape, F32)
    for j in range(NUM_BUCKETS):
        acc = jnp.where(bkt == j, rb_ref[j, h], acc)
    o_ref[...] = jnp.where(valid_ref[...] > 0, acc, NEG)


def _bias_tables(rel_bias):
    bkt, valid = _bucket_tables()
    npat = len(DILATED)
    blk = pl.BlockSpec((None, BLK, 2 * BLK), lambda p, h: (p, 0, 0))
    return pl.pallas_call(
        _bias_kernel,
        grid=(npat, ATTN_HEADS),
        in_specs=[pl.BlockSpec(memory_space=pltpu.SMEM), blk, blk],
        out_specs=pl.BlockSpec((None, None, BLK, 2 * BLK), lambda p, h: (p, h, 0, 0)),
        out_shape=jax.ShapeDtypeStruct((npat, ATTN_HEADS, BLK, 2 * BLK), F32),
        name="bias",
    )(rel_bias, jnp.asarray(bkt), jnp.asarray(valid))


def _in_proj_kernel(x_ref, ln1_ref, win_ref, sgnw_ref, sguw_ref, bst_ref, mixw_ref,
                    qkv_ref, sgu_ref, sgu_s):
    tm = x_ref.shape[0]
    h = _rms_rows(x_ref[...], ln1_ref[...]).astype(BF16)
    proj = jnp.dot(h, win_ref[...], preferred_element_type=F32)
    qkv_ref[...] = proj[:, :3 * D_ATTN]
    row = lax.broadcasted_iota(jnp.int32, (CHUNK, CHUNK), 0)
    col = lax.broadcasted_iota(jnp.int32, (CHUNK, CHUNK), 1)
    for g in range(SGU_GROUPS):
        u_lo = 3 * D_ATTN + g * SGU_CH
        v_lo = 3 * D_ATTN + D_SGU + g * SGU_CH
        ua = jax.nn.gelu(proj[:, u_lo:u_lo + SGU_CH])
        va = jax.nn.gelu(proj[:, v_lo:v_lo + SGU_CH])
        vn = _rms_rows(va, sgnw_ref[g:g + 1, :]).astype(BF16)
        w = jnp.where(row >= col, sguw_ref[g], 0.0).astype(BF16)
        bias = bst_ref[:, g:g + 1]
        for c in range(tm // CHUNK):
            rows = slice(c * CHUNK, (c + 1) * CHUNK)
            spatial = jnp.dot(w, vn[rows], preferred_element_type=F32) + bias
            sgu_s[rows, g * SGU_CH:(g + 1) * SGU_CH] = ua[rows] * spatial
    sgu_ref[...] = _rms_rows(sgu_s[...], mixw_ref[...]).astype(BF16)


def _in_proj(x2, ln1, w_in, sgu_norm_w, sgu_w, sgu_b, mixw_sgu, tm):
    t, d = x2.shape
    dproj = w_in.shape[1]
    full = lambda shape: pl.BlockSpec(shape, lambda i: (0,) * len(shape))
    return pl.pallas_call(
        _in_proj_kernel,
        grid=(t // tm,),
        in_specs=[pl.BlockSpec((tm, d), lambda i: (i, 0)),
                  full((1, d)), full((d, dproj)), full((SGU_GROUPS, SGU_CH)),
                  full((SGU_GROUPS, CHUNK, CHUNK)), full((CHUNK, SGU_GROUPS)), full((1, D_SGU))],
        out_specs=[pl.BlockSpec((tm, 3 * D_ATTN), lambda i: (i, 0)),
                   pl.BlockSpec((tm, D_SGU), lambda i: (i, 0))],
        out_shape=[jax.ShapeDtypeStruct((t, 3 * D_ATTN), F32),
                   jax.ShapeDtypeStruct((t, D_SGU), BF16)],
        scratch_shapes=[pltpu.VMEM((tm, D_SGU), F32)],
        compiler_params=pltpu.CompilerParams(dimension_semantics=("arbitrary",),
                                             vmem_limit_bytes=VMEM_LIMIT),
        name="in_proj",
    )(x2, ln1, w_in, sgu_norm_w, sgu_w, sgu_b.T, mixw_sgu)


def _attn_kernel(q_ref, k_ref, v_ref, qw_ref, kw_ref, bias_ref, o_ref,
                 qs, ks, vs, acc_s, m_s):
    s_len = q_ref.shape[0]
    heads = LANES // HEAD_DIM
    lane = lax.broadcasted_iota(jnp.int32, (s_len, LANES), 1)
    r_i = lax.broadcasted_iota(jnp.int32, (LANES, LANES), 0) // HEAD_DIM
    c_i = lax.broadcasted_iota(jnp.int32, (LANES, LANES), 1) // HEAD_DIM
    averager = jnp.where(r_i == c_i, 1.0 / HEAD_DIM, 0.0).astype(BF16)

    def head_rms(x, w):
        sq = x * x
        hi = sq.astype(BF16)
        lo = (sq - hi.astype(F32)).astype(BF16)
        mean = (jnp.dot(hi, averager, preferred_element_type=F32)
                + jnp.dot(lo, averager, preferred_element_type=F32))
        return x * lax.rsqrt(mean + EPS) * w

    qn = head_rms(q_ref[...], qw_ref[...]) * (HEAD_DIM ** -0.5)
    ks[...] = head_rms(k_ref[...], kw_ref[...])
    v = v_ref[...]
    for hh in range(heads):
        mine = (lane // HEAD_DIM) == hh
        qs[hh] = jnp.where(mine, qn, 0.0)
        vs[hh] = jnp.where(mine, v, 1.0)

    for p, (_, dil) in enumerate(DILATED):
        nblk = s_len // dil // BLK
        shift = dil.bit_length() - 1

        def group(qstarts, first, p=p, dil=dil):
            nk = BLK if first else 2 * BLK
            chains = [(hh, q0) for q0 in qstarts for hh in range(heads)]
            rows = lambda q0, n: pl.ds(q0, n, stride=dil)
            logits = []
            for hh, q0 in chains:
                qb = qs[hh, rows(q0, BLK), :].astype(BF16)
                kb = ks[rows(q0 - (nk - BLK) * dil, nk), :].astype(BF16)
                s = lax.dot_general(qb, kb, (((1,), (1,)), ((), ())), preferred_element_type=F32)
                bias = bias_ref[p, hh, :, BLK:] if first else bias_ref[p, hh]
                logits.append(jnp.where(bias > 0.1 * NEG, s + bias, NEG))
            probs = []
            for (hh, q0), lg in zip(chains, logits):
                m = jnp.max(lg, axis=-1, keepdims=True)
                m_s[hh, p, rows(q0, BLK), :] = jnp.broadcast_to(m, (BLK, LANES))
                probs.append(jnp.exp(lg - m).astype(BF16))
            for (hh, q0), pe in zip(chains, probs):
                vb = vs[hh, rows(q0 - (nk - BLK) * dil, nk), :].astype(BF16)
                acc_s[hh, p, rows(q0, BLK), :] = jnp.dot(pe, vb, preferred_element_type=F32)

        def first_groups(g, carry, group=group):
            group([g * ATTN_GROUP + j for j in range(min(dil, ATTN_GROUP))], True)
            return carry

        def later_groups(g, carry, group=group, dil=dil, shift=shift):
            starts = []
            for j in range(ATTN_GROUP):
                i = dil + g * ATTN_GROUP + j
                starts.append(jnp.bitwise_and(i, dil - 1) + jnp.right_shift(i, shift) * (dil * BLK))
            group(starts, False)
            return carry

        lax.fori_loop(0, -(-dil // ATTN_GROUP), first_groups, 0)
        n_later = (nblk - 1) * dil
        lax.fori_loop(0, n_later // ATTN_GROUP, later_groups, 0)
        tail = [dil + n_later // ATTN_GROUP * ATTN_GROUP + j for j in range(n_later % ATTN_GROUP)]
        if tail:
            group([(i & (dil - 1)) + (i >> shift) * (dil * BLK) for i in tail], False)

    out = None
    for hh in range(heads):
        ms = [m_s[hh, p] for p in range(len(DILATED))]
        m_all = functools.reduce(jnp.maximum, ms)
        tot = sum(jnp.exp(m - m_all) * acc_s[hh, p] for p, m in enumerate(ms))
        ratio = tot / pltpu.roll(tot, HEAD_DIM, axis=1)
        out = ratio if out is None else jnp.where((lane // HEAD_DIM) == hh, ratio, out)
    o_ref[...] = out


def _attention(qkv3, q_norm_w, k_norm_w, bias):
    b, s, _ = qkv3.shape
    nhp = D_ATTN // LANES
    heads = LANES // HEAD_DIM
    npat = len(DILATED)
    col = lambda off: pl.BlockSpec((None, s, LANES), lambda bi, hp, off=off: (bi, 0, off + hp))
    return pl.pallas_call(
        _attn_kernel,
        grid=(b, nhp),
        in_specs=[col(0), col(nhp), col(2 * nhp),
                  pl.BlockSpec((1, LANES), lambda bi, hp: (0, 0)),
                  pl.BlockSpec((1, LANES), lambda bi, hp: (0, 0)),
                  pl.BlockSpec((npat, LANES // HEAD_DIM, BLK, 2 * BLK), lambda bi, hp: (0, hp, 0, 0))],
        out_specs=pl.BlockSpec((None, s, LANES), lambda bi, hp: (bi, 0, hp)),
        out_shape=jax.ShapeDtypeStruct((b, s, D_ATTN), F32),
        scratch_shapes=[pltpu.VMEM((heads, s, LANES), F32), pltpu.VMEM((s, LANES), F32),
                        pltpu.VMEM((heads, s, LANES), F32),
                        pltpu.VMEM((heads, npat, s, LANES), F32),
                        pltpu.VMEM((heads, npat, s, LANES), F32)],
        compiler_params=pltpu.CompilerParams(dimension_semantics=("arbitrary", "arbitrary"),
                                             vmem_limit_bytes=VMEM_LIMIT),
        name="attn",
    )(qkv3, qkv3, qkv3, jnp.tile(q_norm_w, (1, heads)), jnp.tile(k_norm_w, (1, heads)), bias)


def _out_proj_kernel(attn_ref, sgu_ref, x_ref, mixw_ref, wout_ref, ln2_ref, x1_ref, hnt_ref):
    an = _rms_rows(attn_ref[...], mixw_ref[...]).astype(BF16)
    y = jnp.dot(an, wout_ref[:D_ATTN, :], preferred_element_type=F32)
    y = y + jnp.dot(sgu_ref[...], wout_ref[D_ATTN:, :], preferred_element_type=F32)
    x1 = x_ref[...] + y
    x1_ref[...] = x1
    hnt_ref[...] = _rms_rows(x1, ln2_ref[...]).T.astype(BF16)


def _out_proj(attn2, sgu_n, x2, mixw_attn, w_out, ln2, tm):
    t, d = x2.shape
    full = lambda shape: pl.BlockSpec(shape, lambda i: (0,) * len(shape))
    return pl.pallas_call(
        _out_proj_kernel,
        grid=(t // tm,),
        in_specs=[pl.BlockSpec((tm, D_ATTN), lambda i: (i, 0)),
                  pl.BlockSpec((tm, D_SGU), lambda i: (i, 0)),
                  pl.BlockSpec((tm, d), lambda i: (i, 0)),
                  full((1, D_ATTN)), full((D_ATTN + D_SGU, d)), full((1, d))],
        out_specs=[pl.BlockSpec((tm, d), lambda i: (i, 0)),
                   pl.BlockSpec((d, tm), lambda i: (0, i))],
        out_shape=[jax.ShapeDtypeStruct((t, d), F32), jax.ShapeDtypeStruct((d, t), BF16)],
        compiler_params=pltpu.CompilerParams(dimension_semantics=("arbitrary",),
                                             vmem_limit_bytes=VMEM_LIMIT),
        name="out_proj",
    )(attn2, sgu_n, x2, mixw_attn, w_out, ln2)


def _extract16(x, order, tie_break):
    rank = jnp.full(x.shape, PEER_TOPK, jnp.int32)
    vals = []
    cur = x
    for r in range(PEER_TOPK):
        m = jnp.max(cur, axis=0, keepdims=True)
        sel = cur == m
        if tie_break:
            first = jnp.min(jnp.where(sel, order, jnp.iinfo(jnp.int32).max), axis=0, keepdims=True)
            sel = order == first
        rank = jnp.where(sel, r, rank)
        cur = jnp.where(sel, -jnp.inf, cur)
        vals.append(m)
    return rank, vals


def _taken(rank):
    return jnp.sum(jnp.where(rank < PEER_TOPK, 1.0, 0.0))


def _staircase_counts(v1, v2, tb, tie_break):
    v2_all = jnp.concatenate(v2, axis=0)
    cands, poss, sizes = [], [], []
    for r1 in range(PEER_TOPK):
        rows = PEER_TOPK if r1 == 0 else 8
        r2 = lax.broadcasted_iota(jnp.int32, (rows, tb), 0)
        cands.append(jnp.where(r2 < PEER_TOPK // (r1 + 1), v1[r1] + v2_all[:rows], -jnp.inf))
        poss.append(r2 + r1 * PEER_TOPK)
        sizes.append(rows)
    cand = jnp.concatenate(cands, axis=0)
    pos = jnp.concatenate(poss, axis=0)
    rank, _ = _extract16(cand, pos, tie_break)
    picked = jnp.where(rank < PEER_TOPK, 1.0, 0.0)
    z = jnp.sum(picked * jnp.exp(cand - (v1[0] + v2[0])), axis=0, keepdims=True)
    counts, lo = [], 0
    for rows in sizes:
        counts.append(jnp.sum(picked[lo:lo + rows], axis=0, keepdims=True))
        lo += rows
    return counts, z, _taken(rank)


def _head_codes(s1, s2, tie_break):
    tb = s1.shape[1]
    iota = lax.broadcasted_iota(jnp.int32, s1.shape, 0)
    rank1, v1 = _extract16(s1, iota, tie_break)
    rank2, v2 = _extract16(s2, iota, tie_break)
    counts, z, taken = _staircase_counts(v1, v2, tb, tie_break)
    c1 = jnp.zeros(s1.shape, F32)
    for r1 in range(PEER_TOPK):
        c1 = jnp.where(rank1 == r1, counts[r1], c1)
    q1 = jnp.exp(s1 - v1[0]) / z
    c2 = rank2.astype(F32).astype(BF16)
    p2 = jnp.exp(s2 - v2[0]).astype(BF16)
    return (c1, q1, c2, p2), taken + _taken(rank1) + _taken(rank2)


def _route_kernel(hnt_ref, wqt_ref, k1_ref, k2_ref, c1_ref, q1_ref, c2_ref, p2_ref, qt_s):
    tb = hnt_ref.shape[1]
    half = PEER_D_KEY // 2
    qt_s[...] = jnp.dot(wqt_ref[...], hnt_ref[...], preferred_element_type=F32)

    def scores(h):
        base = pl.multiple_of(h * PEER_D_KEY, PEER_D_KEY)
        qa = qt_s[pl.ds(base, half), :].astype(BF16)
        qb = qt_s[pl.ds(base + half, half), :].astype(BF16)
        return (jnp.dot(k1_ref[h], qa, preferred_element_type=F32),
                jnp.dot(k2_ref[h], qb, preferred_element_type=F32))

    def heads(g, carry):
        hs = [g * ROUTE_HEADS + j for j in range(ROUTE_HEADS)]
        ss = [scores(h) for h in hs]
        fast = [_head_codes(s1, s2, tie_break=False) for s1, s2 in ss]
        for h, (s1, s2), (codes, taken) in zip(hs, ss, fast):
            untied = taken == float(3 * PEER_TOPK * tb)
            codes = lax.cond(untied, lambda codes=codes: codes,
                             lambda s1=s1, s2=s2: _head_codes(s1, s2, tie_break=True)[0])
            for ref, code in zip((c1_ref, q1_ref, c2_ref, p2_ref), codes):
                ref[h] = code
        return carry

    lax.fori_loop(0, PEER_HEADS // ROUTE_HEADS, heads, 0)


def _route(hnt, wqt, keys1, keys2, tb):
    d, t = hnt.shape
    nq = wqt.shape[0]
    full = lambda shape: pl.BlockSpec(shape, lambda i: (0,) * len(shape))
    code = pl.BlockSpec((PEER_HEADS, PEER_KEYS, tb), lambda i: (0, 0, i))
    code_shape = lambda dt: jax.ShapeDtypeStruct((PEER_HEADS, PEER_KEYS, t), dt)
    return pl.pallas_call(
        _route_kernel,
        grid=(t // tb,),
        in_specs=[pl.BlockSpec((d, tb), lambda i: (0, i)), full((nq, d)),
                  full(keys1.shape), full(keys2.shape)],
        out_specs=[code] * 4,
        out_shape=[code_shape(F32), code_shape(F32), code_shape(BF16), code_shape(BF16)],
        scratch_shapes=[pltpu.VMEM((nq, tb), F32)],
        compiler_params=pltpu.CompilerParams(dimension_semantics=("arbitrary",),
                                             vmem_limit_bytes=VMEM_LIMIT),
        name="route",
    )(hnt, wqt, keys1, keys2)


def _experts_kernel(hnt_ref, u_ref, vt_ref, c1_ref, q1_ref, c2_ref, p2_ref, x1_ref, o_ref,
                    acc_s, st_s, z_s, *, n_eb, n_blocks):
    t = pl.program_id(0)
    eb = u_ref.shape[0]
    d_model = vt_ref.shape[0]
    keys_per_block = eb // PEER_KEYS
    j_b = jnp.clip(t - 1, 0, n_blocks - 1) % n_eb
    j_c = jnp.clip(t - 2, 0, n_blocks - 1) % n_eb

    @pl.when(t == 0)
    def _():
        st_s[...] = jnp.zeros_like(st_s)
        z_s[...] = jnp.zeros_like(z_s)
        acc_s[...] = jnp.zeros_like(acc_s)

    @pl.when(jnp.logical_and(t >= 2, j_c == 0))
    def _():
        acc_s[...] = jnp.zeros_like(acc_s)

    def stages(slot):
        rows_of = lambda ref, ii: [ref[h, pl.ds(j_b * keys_per_block + ii, 1), :].astype(BF16)
                                   for h in range(PEER_HEADS)]
        c1_all = [rows_of(c1_ref, ii) for ii in range(keys_per_block)]
        q1_all = [rows_of(q1_ref, ii) for ii in range(keys_per_block)]
        n_cc = st_s.shape[2] // GATE_TILE

        def stage_a(half, cc):
            rows = slice(half * (eb // 2), (half + 1) * (eb // 2))
            cols = slice(cc * GATE_TILE, (cc + 1) * GATE_TILE)
            st_s[slot, rows, cols] = jnp.dot(u_ref[rows, :], hnt_ref[:, cols],
                                             preferred_element_type=F32)

        def stage_b(ii, kh, cc):
            keys = slice(kh * GATE_ROWS, (kh + 1) * GATE_ROWS)
            rows = slice(ii * PEER_KEYS + kh * GATE_ROWS, ii * PEER_KEYS + (kh + 1) * GATE_ROWS)
            cols = slice(cc * GATE_TILE, (cc + 1) * GATE_TILE)
            gate = None
            for h in range(PEER_HEADS):
                term = jnp.where(c2_ref[h, keys, cols] < c1_all[ii][h][:, cols],
                                 p2_ref[h, keys, cols] * q1_all[ii][h][:, cols], jnp.zeros((), BF16))
                gate = term if gate is None else gate + term
            z_s[slot, rows, cols] = jax.nn.gelu(st_s[1 - slot, rows, cols]).astype(BF16) * gate

        def stage_c(quarter, cc):
            rows = slice(quarter * (d_model // 4), (quarter + 1) * (d_model // 4))
            cols = slice(cc * GATE_TILE, (cc + 1) * GATE_TILE)
            acc_s[rows, cols] += jnp.dot(vt_ref[rows, :], z_s[1 - slot, :, cols],
                                         preferred_element_type=F32)

        b_tiles = [(ii, kh, cc) for cc in range(n_cc) for ii in range(keys_per_block)
                   for kh in range(PEER_KEYS // GATE_ROWS)]
        mxu_chunks = ([functools.partial(stage_a, half, cc) for cc in range(n_cc) for half in range(2)]
                      + [functools.partial(stage_c, q, cc) for cc in range(n_cc) for q in range(4)])
        for k, chunk in enumerate(mxu_chunks):
            chunk()
            for tile in b_tiles[k * len(b_tiles) // len(mxu_chunks):
                                (k + 1) * len(b_tiles) // len(mxu_chunks)]:
                stage_b(*tile)

    for parity in range(2):
        pl.when(t % 2 == parity)(functools.partial(stages, parity))

    @pl.when(jnp.logical_and(t >= 2, j_c == n_eb - 1))
    def _():
        o_ref[...] = x1_ref[...] + acc_s[...].T


def _experts(hnt, u_tab, vt_blocks, codes, x1, tb):
    d, t = hnt.shape
    n_eb, _, eb = vt_blocks.shape
    n_blocks = (t // tb) * n_eb
    g_a = lambda s: jnp.minimum(s, n_blocks - 1)
    g_b = lambda s: jnp.clip(s - 1, 0, n_blocks - 1)
    g_c = lambda s: jnp.clip(s - 2, 0, n_blocks - 1)
    code = pl.BlockSpec((PEER_HEADS, PEER_KEYS, tb), lambda s: (0, 0, g_b(s) // n_eb))
    assert codes[0].dtype == F32 and codes[1].dtype == F32
    assert codes[2].dtype == BF16 and codes[3].dtype == BF16
    return pl.pallas_call(
        functools.partial(_experts_kernel, n_eb=n_eb, n_blocks=n_blocks),
        grid=(n_blocks + 2,),
        in_specs=[pl.BlockSpec((d, tb), lambda s: (0, g_a(s) // n_eb)),
                  pl.BlockSpec((eb, d), lambda s: (g_a(s) % n_eb, 0)),
                  pl.BlockSpec((None, d, eb), lambda s: (g_c(s) % n_eb, 0, 0)),
                  code, code, code, code,
                  pl.BlockSpec((tb, d), lambda s: (g_c(s) // n_eb, 0))],
        out_specs=pl.BlockSpec((tb, d), lambda s: (g_c(s) // n_eb, 0)),
        out_shape=jax.ShapeDtypeStruct((t, d), F32),
        scratch_shapes=[pltpu.VMEM((d, tb), F32), pltpu.VMEM((2, eb, tb), F32),
                        pltpu.VMEM((2, eb, tb), BF16)],
        compiler_params=pltpu.CompilerParams(dimension_semantics=("arbitrary",),
                                             vmem_limit_bytes=VMEM_LIMIT),
        name="experts",
    )(hnt, u_tab, vt_blocks, *codes, x1)


def kernel(x, ln1_w, w_in, q_norm_w, k_norm_w, sgu_norm_w, sgu_w, sgu_b, mix_norm_w, w_out,
           ln2_w, peer_w_query, peer_keys1, peer_keys2, peer_u, peer_v, rel_bias):
    b, s, d = x.shape
    t = b * s
    bias = _bias_tables(rel_bias)
    for l in range(ln1_w.shape[0]):
        x2 = x.reshape(t, d)
        qkv, sgu_n = _in_proj(x2, ln1_w[l][None], w_in[l].astype(BF16), sgu_norm_w[l], sgu_w[l],
                              sgu_b[l], mix_norm_w[l, D_ATTN:][None], tm=512)
        attn = _attention(qkv.reshape(b, s, 3 * D_ATTN), q_norm_w[l][None], k_norm_w[l][None], bias)
        x1, hnt = _out_proj(attn.reshape(t, D_ATTN), sgu_n, x2, mix_norm_w[l, :D_ATTN][None],
                            w_out[l].astype(BF16), ln2_w[l][None], tm=512)
        codes = _route(hnt, peer_w_query[l].T.astype(BF16), peer_keys1[l].astype(BF16),
                       peer_keys2[l].astype(BF16), tb=256)
        vt_blocks = peer_v[l].reshape(-1, EXPERT_BLOCK, d).transpose(0, 2, 1).astype(BF16)
        out = _experts(hnt, peer_u[l].astype(BF16), vt_blocks, codes, x1, tb=EXPERT_TOKENS)
        x = out.reshape(b, s, d)
    return x
```

```python
import functools
import math

import numpy as np
import jax
import jax.numpy as jnp
from jax import lax
from jax.experimental import pallas as pl
from jax.experimental.pallas import tpu as pltpu

F32 = jnp.float32
BF16 = jnp.bfloat16

ATTN_HEADS = 8
HEAD_DIM = 64
D_ATTN = ATTN_HEADS * HEAD_DIM
SGU_GROUPS = 4
SGU_CH = 128
D_SGU = SGU_GROUPS * SGU_CH
CHUNK = 128
DILATED = ((128, 1), (512, 4), (2048, 16))
BLK = 128
NUM_BUCKETS = 32
MAX_DISTANCE = 2048
PEER_HEADS = 8
PEER_KEYS = 128
PEER_D_KEY = 256
PEER_TOPK = 16
EPS = 1e-6
NEG = -1e30

LANES = 128
ATTN_GROUP = 8
ROUTE_HEADS = 2
EXPERT_BLOCK = 512
EXPERT_TOKENS = 512
GATE_TILE = 2 * LANES
GATE_ROWS = 64
VMEM_LIMIT = 56 * 1024 * 1024


def _rms_rows(x, w):
    return x * lax.rsqrt(jnp.mean(x * x, axis=-1, keepdims=True) + EPS) * w


def _bucket_tables():
    qi = np.arange(BLK)[:, None]
    kj = np.arange(2 * BLK)[None, :]
    rel = BLK + qi - kj
    max_exact = NUM_BUCKETS // 2
    buckets, valid = [], []
    for window, dil in DILATED:
        w_sub = window // dil
        dist = np.maximum(rel, 0) * dil
        d_f = np.maximum(dist, max_exact).astype(np.float32)
        large = max_exact + (np.log(d_f / np.float32(max_exact)) / np.float32(math.log(MAX_DISTANCE / max_exact))
                             * np.float32(NUM_BUCKETS - max_exact)).astype(np.int32)
        large = np.minimum(large, NUM_BUCKETS - 1)
        buckets.append(np.where(dist < max_exact, dist, large).astype(np.int32))
        valid.append(((rel >= 0) & (rel <= w_sub)).astype(np.int32))
    return np.stack(buckets), np.stack(valid)


def _bias_kernel(rb_ref, bkt_ref, valid_ref, o_ref):
    h = pl.program_id(1)
    bkt = bkt_ref[...]
    acc = jnp.zeros(bkt.shape, F32)
    for j in range(NUM_BUCKETS):
        acc = jnp.where(bkt == j, rb_ref[j, h], acc)
    o_ref[...] = jnp.where(valid_ref[...] > 0, acc, NEG)


def _bias_tables(rel_bias):
    bkt, valid = _bucket_tables()
    npat = len(DILATED)
    blk = pl.BlockSpec((None, BLK, 2 * BLK), lambda p, h: (p, 0, 0))
    return pl.pallas_call(
        _bias_kernel,
        grid=(npat, ATTN_HEADS),
        in_specs=[pl.BlockSpec(memory_space=pltpu.SMEM), blk, blk],
        out_specs=pl.BlockSpec((None, None, BLK, 2 * BLK), lambda p, h: (p, h, 0, 0)),
        out_shape=jax.ShapeDtypeStruct((npat, ATTN_HEADS, BLK, 2 * BLK), F32),
        name="bias",
    )(rel_bias, jnp.asarray(bkt), jnp.asarray(valid))


def _in_proj_kernel(x_ref, ln1_ref, win_ref, sgnw_ref, sguw_ref, bst_ref, mixw_ref,
                    qkv_ref, sgu_ref, sgu_s):
    tm = x_ref.shape[0]
    h = _rms_rows(x_ref[...], ln1_ref[...]).astype(BF16)
    proj = jnp.dot(h, win_ref[...], preferred_element_type=F32)
    qkv_ref[...] = proj[:, :3 * D_ATTN]
    row = lax.broadcasted_iota(jnp.int32, (CHUNK, CHUNK), 0)
    col = lax.broadcasted_iota(jnp.int32, (CHUNK, CHUNK), 1)
    for g in range(SGU_GROUPS):
        u_lo = 3 * D_ATTN + g * SGU_CH
        v_lo = 3 * D_ATTN + D_SGU + g * SGU_CH
        ua = jax.nn.gelu(proj[:, u_lo:u_lo + SGU_CH])
        va = jax.nn.gelu(proj[:, v_lo:v_lo + SGU_CH])
        vn = _rms_rows(va, sgnw_ref[g:g + 1, :]).astype(BF16)
        w = jnp.where(row >= col, sguw_ref[g], 0.0).astype(BF16)
        bias = bst_ref[:, g:g + 1]
        for c in range(tm // CHUNK):
            rows = slice(c * CHUNK, (c + 1) * CHUNK)
            spatial = jnp.dot(w, vn[rows], preferred_element_type=F32) + bias
            sgu_s[rows, g * SGU_CH:(g + 1) * SGU_CH] = ua[rows] * spatial
    sgu_ref[...] = _rms_rows(sgu_s[...], mixw_ref[...]).astype(BF16)


def _in_proj(x2, ln1, w_in, sgu_norm_w, sgu_w, sgu_b, mixw_sgu, tm):
    t, d = x2.shape
    dproj = w_in.shape[1]
    full = lambda shape: pl.BlockSpec(shape, lambda i: (0,) * len(shape))
    return pl.pallas_call(
        _in_proj_kernel,
        grid=(t // tm,),
        in_specs=[pl.BlockSpec((tm, d), lambda i: (i, 0)),
                  full((1, d)), full((d, dproj)), full((SGU_GROUPS, SGU_CH)),
                  full((SGU_GROUPS, CHUNK, CHUNK)), full((CHUNK, SGU_GROUPS)), full((1, D_SGU))],
        out_specs=[pl.BlockSpec((tm, 3 * D_ATTN), lambda i: (i, 0)),
                   pl.BlockSpec((tm, D_SGU), lambda i: (i, 0))],
        out_shape=[jax.ShapeDtypeStruct((t, 3 * D_ATTN), F32),
                   jax.ShapeDtypeStruct((t, D_SGU), BF16)],
        scratch_shapes=[pltpu.VMEM((tm, D_SGU), F32)],
        compiler_params=pltpu.CompilerParams(dimension_semantics=("arbitrary",),
                                             vmem_limit_bytes=VMEM_LIMIT),
        name="in_proj",
    )(x2, ln1, w_in, sgu_norm_w, sgu_w, sgu_b.T, mixw_sgu)


def _attn_kernel(q_ref, k_ref, v_ref, qw_ref, kw_ref, bias_ref, o_ref,
                 qs, ks, vs, acc_s, m_s):
    s_len = q_ref.shape[0]
    heads = LANES // HEAD_DIM
    lane = lax.broadcasted_iota(jnp.int32, (s_len, LANES), 1)
    r_i = lax.broadcasted_iota(jnp.int32, (LANES, LANES), 0) // HEAD_DIM
    c_i = lax.broadcasted_iota(jnp.int32, (LANES, LANES), 1) // HEAD_DIM
    averager = jnp.where(r_i == c_i, 1.0 / HEAD_DIM, 0.0).astype(BF16)

    def head_rms(x, w):
        sq = x * x
        hi = sq.astype(BF16)
        lo = (sq - hi.astype(F32)).astype(BF16)
        mean = (jnp.dot(hi, averager, preferred_element_type=F32)
                + jnp.dot(lo, averager, preferred_element_type=F32))
        return x * lax.rsqrt(mean + EPS) * w

    qn = head_rms(q_ref[...], qw_ref[...]) * (HEAD_DIM ** -0.5)
    ks[...] = head_rms(k_ref[...], kw_ref[...])
    v = v_ref[...]
    for hh in range(heads):
        mine = (lane // HEAD_DIM) == hh
        qs[hh] = jnp.where(mine, qn, 0.0)
        vs[hh] = jnp.where(mine, v, 1.0)

    for p, (_, dil) in enumerate(DILATED):
        nblk = s_len // dil // BLK
        shift = dil.bit_length() - 1

        def group(qstarts, first, p=p, dil=dil):
            nk = BLK if first else 2 * BLK
            chains = [(hh, q0) for q0 in qstarts for hh in range(heads)]
            rows = lambda q0, n: pl.ds(q0, n, stride=dil)
            logits = []
            for hh, q0 in chains:
                qb = qs[hh, rows(q0, BLK), :].astype(BF16)
                kb = ks[rows(q0 - (nk - BLK) * dil, nk), :].astype(BF16)
                s = lax.dot_general(qb, kb, (((1,), (1,)), ((), ())), preferred_element_type=F32)
                bias = bias_ref[p, hh, :, BLK:] if first else bias_ref[p, hh]
                logits.append(jnp.where(bias > 0.1 * NEG, s + bias, NEG))
            probs = []
            for (hh, q0), lg in zip(chains, logits):
                m = jnp.max(lg, axis=-1, keepdims=True)
                m_s[hh, p, rows(q0, BLK), :] = jnp.broadcast_to(m, (BLK, LANES))
                probs.append(jnp.exp(lg - m).astype(BF16))
            for (hh, q0), pe in zip(chains, probs):
                vb = vs[hh, rows(q0 - (nk - BLK) * dil, nk), :].astype(BF16)
                acc_s[hh, p, rows(q0, BLK), :] = jnp.dot(pe, vb, preferred_element_type=F32)

        def first_groups(g, carry, group=group):
            group([g * ATTN_GROUP + j for j in range(min(dil, ATTN_GROUP))], True)
            return carry

        def later_groups(g, carry, group=group, dil=dil, shift=shift):
            starts = []
            for j in range(ATTN_GROUP):
                i = dil + g * ATTN_GROUP + j
                starts.append(jnp.bitwise_and(i, dil - 1) + jnp.right_shift(i, shift) * (dil * BLK))
            group(starts, False)
            return carry

        lax.fori_loop(0, -(-dil // ATTN_GROUP), first_groups, 0)
        n_later = (nblk - 1) * dil
        lax.fori_loop(0, n_later // ATTN_GROUP, later_groups, 0)
        tail = [dil + n_later // ATTN_GROUP * ATTN_GROUP + j for j in range(n_later % ATTN_GROUP)]
        if tail:
            group([(i & (dil - 1)) + (i >> shift) * (dil * BLK) for i in tail], False)

    out = None
    for hh in range(heads):
        ms = [m_s[hh, p] for p in range(len(DILATED))]
        m_all = functools.reduce(jnp.maximum, ms)
        tot = sum(jnp.exp(m - m_all) * acc_s[hh, p] for p, m in enumerate(ms))
        ratio = tot / pltpu.roll(tot, HEAD_DIM, axis=1)
        out = ratio if out is None else jnp.where((lane // HEAD_DIM) == hh, ratio, out)
    o_ref[...] = out


def _attention(qkv3, q_norm_w, k_norm_w, bias):
    b, s, _ = qkv3.shape
    nhp = D_ATTN // LANES
    heads = LANES // HEAD_DIM
    npat = len(DILATED)
    col = lambda off: pl.BlockSpec((None, s, LANES), lambda bi, hp, off=off: (bi, 0, off + hp))
    return pl.pallas_call(
        _attn_kernel,
        grid=(b, nhp),
        in_specs=[col(0), col(nhp), col(2 * nhp),
                  pl.BlockSpec((1, LANES), lambda bi, hp: (0, 0)),
                  pl.BlockSpec((1, LANES), lambda bi, hp: (0, 0)),
                  pl.BlockSpec((npat, LANES // HEAD_DIM, BLK, 2 * BLK), lambda bi, hp: (0, hp, 0, 0))],
        out_specs=pl.BlockSpec((None, s, LANES), lambda bi, hp: (bi, 0, hp)),
        out_shape=jax.ShapeDtypeStruct((b, s, D_ATTN), F32),
        scratch_shapes=[pltpu.VMEM((heads, s, LANES), F32), pltpu.VMEM((s, LANES), F32),
                        pltpu.VMEM((heads, s, LANES), F32),
                        pltpu.VMEM((heads, npat, s, LANES), F32),
                        pltpu.VMEM((heads, npat, s, LANES), F32)],
        compiler_params=pltpu.CompilerParams(dimension_semantics=("arbitrary", "arbitrary"),
                                             vmem_limit_bytes=VMEM_LIMIT),
        name="attn",
    )(qkv3, qkv3, qkv3, jnp.tile(q_norm_w, (1, heads)), jnp.tile(k_norm_w, (1, heads)), bias)


def _out_proj_kernel(attn_ref, sgu_ref, x_ref, mixw_ref, wout_ref, ln2_ref, x1_ref, hnt_ref):
    an = _rms_rows(attn_ref[...], mixw_ref[...]).astype(BF16)
    y = jnp.dot(an, wout_ref[:D_ATTN, :], preferred_element_type=F32)
    y = y + jnp.dot(sgu_ref[...], wout_ref[D_ATTN:, :], preferred_element_type=F32)
    x1 = x_ref[...] + y
    x1_ref[...] = x1
    hnt_ref[...] = _rms_rows(x1, ln2_ref[...]).T.astype(BF16)


def _out_proj(attn2, sgu_n, x2, mixw_attn, w_out, ln2, tm):
    t, d = x2.shape
    full = lambda shape: pl.BlockSpec(shape, lambda i: (0,) * len(shape))
    return pl.pallas_call(
        _out_proj_kernel,
        grid=(t // tm,),
        in_specs=[pl.BlockSpec((tm, D_ATTN), lambda i: (i, 0)),
                  pl.BlockSpec((tm, D_SGU), lambda i: (i, 0)),
                  pl.BlockSpec((tm, d), lambda i: (i, 0)),
                  full((1, D_ATTN)), full((D_ATTN + D_SGU, d)), full((1, d))],
        out_specs=[pl.BlockSpec((tm, d), lambda i: (i, 0)),
                   pl.BlockSpec((d, tm), lambda i: (0, i))],
        out_shape=[jax.ShapeDtypeStruct((t, d), F32), jax.ShapeDtypeStruct((d, t), BF16)],
        compiler_params=pltpu.CompilerParams(dimension_semantics=("arbitrary",),
                                             vmem_limit_bytes=VMEM_LIMIT),
        name="out_proj",
    )(attn2, sgu_n, x2, mixw_attn, w_out, ln2)


def _extract16(x, order, tie_break):
    rank = jnp.full(x.shape, PEER_TOPK, jnp.int32)
    vals = []
    cur = x
    for r in range(PEER_TOPK):
        m = jnp.max(cur, axis=0, keepdims=True)
        sel = cur == m
        if tie_break:
            first = jnp.min(jnp.where(sel, order, jnp.iinfo(jnp.int32).max), axis=0, keepdims=True)
            sel = order == first
        rank = jnp.where(sel, r, rank)
        cur = jnp.where(sel, -jnp.inf, cur)
        vals.append(m)
    return rank, vals


def _taken(rank):
    return jnp.sum(jnp.where(rank < PEER_TOPK, 1.0, 0.0))


def _staircase_counts(v1, v2, tb, tie_break):
    v2_all = jnp.concatenate(v2, axis=0)
    cands, poss, sizes = [], [], []
    for r1 in range(PEER_TOPK):
        rows = PEER_TOPK if r1 == 0 else 8
        r2 = lax.broadcasted_iota(jnp.int32, (rows, tb), 0)
        cands.append(jnp.where(r2 < PEER_TOPK // (r1 + 1), v1[r1] + v2_all[:rows], -jnp.inf))
        poss.append(r2 + r1 * PEER_TOPK)
        sizes.append(rows)
    cand = jnp.concatenate(cands, axis=0)
    pos = jnp.concatenate(poss, axis=0)
    rank, _ = _extract16(cand, pos, tie_break)
    picked = jnp.where(rank < PEER_TOPK, 1.0, 0.0)
    z = jnp.sum(picked * jnp.exp(cand - (v1[0] + v2[0])), axis=0, keepdims=True)
    counts, lo = [], 0
    for rows in sizes:
        counts.append(jnp.sum(picked[lo:lo + rows], axis=0, keepdims=True))
        lo += rows
    return counts, z, _taken(rank)


def _head_codes(s1, s2, tie_break):
    tb = s1.shape[1]
    iota = lax.broadcasted_iota(jnp.int32, s1.shape, 0)
    rank1, v1 = _extract16(s1, iota, tie_break)
    rank2, v2 = _extract16(s2, iota, tie_break)
    counts, z, taken = _staircase_counts(v1, v2, tb, tie_break)
    c1 = jnp.zeros(s1.shape, F32)
    for r1 in range(PEER_TOPK):
        c1 = jnp.where(rank1 == r1, counts[r1], c1)
    q1 = jnp.exp(s1 - v1[0]) / z
    c2 = rank2.astype(F32).astype(BF16)
    p2 = jnp.exp(s2 - v2[0]).astype(BF16)
    return (c1, q1, c2, p2), taken + _taken(rank1) + _taken(rank2)


def _route_kernel(hnt_ref, wqt_ref, k1_ref, k2_ref, c1_ref, q1_ref, c2_ref, p2_ref, qt_s):
    tb = hnt_ref.shape[1]
    half = PEER_D_KEY // 2
    qt_s[...] = jnp.dot(wqt_ref[...], hnt_ref[...], preferred_element_type=F32)

    def scores(h):
        base = pl.multiple_of(h * PEER_D_KEY, PEER_D_KEY)
        qa = qt_s[pl.ds(base, half), :].astype(BF16)
        qb = qt_s[pl.ds(base + half, half), :].astype(BF16)
        return (jnp.dot(k1_ref[h], qa, preferred_element_type=F32),
                jnp.dot(k2_ref[h], qb, preferred_element_type=F32))

    def heads(g, carry):
        hs = [g * ROUTE_HEADS + j for j in range(ROUTE_HEADS)]
        ss = [scores(h) for h in hs]
        fast = [_head_codes(s1, s2, tie_break=False) for s1, s2 in ss]
        for h, (s1, s2), (codes, taken) in zip(hs, ss, fast):
            untied = taken == float(3 * PEER_TOPK * tb)
            codes = lax.cond(untied, lambda codes=codes: codes,
                             lambda s1=s1, s2=s2: _head_codes(s1, s2, tie_break=True)[0])
            for ref, code in zip((c1_ref, q1_ref, c2_ref, p2_ref), codes):
                ref[h] = code
        return carry

    lax.fori_loop(0, PEER_HEADS // ROUTE_HEADS, heads, 0)


def _route(hnt, wqt, keys1, keys2, tb):
    d, t = hnt.shape
    nq = wqt.shape[0]
    full = lambda shape: pl.BlockSpec(shape, lambda i: (0,) * len(shape))
    code = pl.BlockSpec((PEER_HEADS, PEER_KEYS, tb), lambda i: (0, 0, i))
    code_shape = lambda dt: jax.ShapeDtypeStruct((PEER_HEADS, PEER_KEYS, t), dt)
    return pl.pallas_call(
        _route_kernel,
        grid=(t // tb,),
        in_specs=[pl.BlockSpec((d, tb), lambda i: (0, i)), full((nq, d)),
                  full(keys1.shape), full(keys2.shape)],
        out_specs=[code] * 4,
        out_shape=[code_shape(F32), code_shape(F32), code_shape(BF16), code_shape(BF16)],
        scratch_shapes=[pltpu.VMEM((nq, tb), F32)],
        compiler_params=pltpu.CompilerParams(dimension_semantics=("arbitrary",),
                                             vmem_limit_bytes=VMEM_LIMIT),
        name="route",
    )(hnt, wqt, keys1, keys2)


def _experts_kernel(hnt_ref, u_ref, vt_ref, c1_ref, q1_ref, c2_ref, p2_ref, x1_ref, o_ref,
                    acc_s, st_s, z_s, *, n_eb, n_blocks):
    t = pl.program_id(0)
    eb = u_ref.shape[0]
    d_model = vt_ref.shape[0]
    keys_per_block = eb // PEER_KEYS
    j_b = jnp.clip(t - 1, 0, n_blocks - 1) % n_eb
    j_c = jnp.clip(t - 2, 0, n_blocks - 1) % n_eb

    @pl.when(t == 0)
    def _():
        st_s[...] = jnp.zeros_like(st_s)
        z_s[...] = jnp.zeros_like(z_s)
        acc_s[...] = jnp.zeros_like(acc_s)

    @pl.when(jnp.logical_and(t >= 2, j_c == 0))
    def _():
        acc_s[...] = jnp.zeros_like(acc_s)

    def stages(slot):
        rows_of = lambda ref, ii: [ref[h, pl.ds(j_b * keys_per_block + ii, 1), :].astype(BF16)
                                   for h in range(PEER_HEADS)]
        c1_all = [rows_of(c1_ref, ii) for ii in range(keys_per_block)]
        q1_all = [rows_of(q1_ref, ii) for ii in range(keys_per_block)]
        n_cc = st_s.shape[2] // GATE_TILE

        def stage_a(half, cc):
            rows = slice(half * (eb // 2), (half + 1) * (eb // 2))
            cols = slice(cc * GATE_TILE, (cc + 1) * GATE_TILE)
            st_s[slot, rows, cols] = jnp.dot(u_ref[rows, :], hnt_ref[:, cols],
                                             preferred_element_type=F32)

        def stage_b(ii, kh, cc):
            keys = slice(kh * GATE_ROWS, (kh + 1) * GATE_ROWS)
            rows = slice(ii * PEER_KEYS + kh * GATE_ROWS, ii * PEER_KEYS + (kh + 1) * GATE_ROWS)
            cols = slice(cc * GATE_TILE, (cc + 1) * GATE_TILE)
            gate = None
            for h in range(PEER_HEADS):
                term = jnp.where(c2_ref[h, keys, cols] < c1_all[ii][h][:, cols],
                                 p2_ref[h, keys, cols] * q1_all[ii][h][:, cols], jnp.zeros((), BF16))
                gate = term if gate is None else gate + term
            z_s[slot, rows, cols] = jax.nn.gelu(st_s[1 - slot, rows, cols]).astype(BF16) * gate

        def stage_c(quarter, cc):
            rows = slice(quarter * (d_model // 4), (quarter + 1) * (d_model // 4))
            cols = slice(cc * GATE_TILE, (cc + 1) * GATE_TILE)
            acc_s[rows, cols] += jnp.dot(vt_ref[rows, :], z_s[1 - slot, :, cols],
                                         preferred_element_type=F32)

        b_tiles = [(ii, kh, cc) for cc in range(n_cc) for ii in range(keys_per_block)
                   for kh in range(PEER_KEYS // GATE_ROWS)]
        mxu_chunks = ([functools.partial(stage_a, half, cc) for cc in range(n_cc) for half in range(2)]
                      + [functools.partial(stage_c, q, cc) for cc in range(n_cc) for q in range(4)])
        for k, chunk in enumerate(mxu_chunks):
            chunk()
            for tile in b_tiles[k * len(b_tiles) // len(mxu_chunks):
                                (k + 1) * len(b_tiles) // len(mxu_chunks)]:
                stage_b(*tile)

    for parity in range(2):
        pl.when(t % 2 == parity)(functools.partial(stages, parity))

    @pl.when(jnp.logical_and(t >= 2, j_c == n_eb - 1))
    def _():
        o_ref[...] = x1_ref[...] + acc_s[...].T


def _experts(hnt, u_tab, vt_blocks, codes, x1, tb):
    d, t = hnt.shape
    n_eb, _, eb = vt_blocks.shape
    n_blocks = (t // tb) * n_eb
    g_a = lambda s: jnp.minimum(s, n_blocks - 1)
    g_b = lambda s: jnp.clip(s - 1, 0, n_blocks - 1)
    g_c = lambda s: jnp.clip(s - 2, 0, n_blocks - 1)
    code = pl.BlockSpec((PEER_HEADS, PEER_KEYS, tb), lambda s: (0, 0, g_b(s) // n_eb))
    assert codes[0].dtype == F32 and codes[1].dtype == F32
    assert codes[2].dtype == BF16 and codes[3].dtype == BF16
    return pl.pallas_call(
        functools.partial(_experts_kernel, n_eb=n_eb, n_blocks=n_blocks),
        grid=(n_blocks + 2,),
        in_specs=[pl.BlockSpec((d, tb), lambda s: (0, g_a(s) // n_eb)),
                  pl.BlockSpec((eb, d), lambda s: (g_a(s) % n_eb, 0)),
                  pl.BlockSpec((None, d, eb), lambda s: (g_c(s) % n_eb, 0, 0)),
                  code, code, code, code,
                  pl.BlockSpec((tb, d), lambda s: (g_c(s) // n_eb, 0))],
        out_specs=pl.BlockSpec((tb, d), lambda s: (g_c(s) // n_eb, 0)),
        out_shape=jax.ShapeDtypeStruct((t, d), F32),
        scratch_shapes=[pltpu.VMEM((d, tb), F32), pltpu.VMEM((2, eb, tb), F32),
                        pltpu.VMEM((2, eb, tb), BF16)],
        compiler_params=pltpu.CompilerParams(dimension_semantics=("arbitrary",),
                                             vmem_limit_bytes=VMEM_LIMIT),
        name="experts",
    )(hnt, u_tab, vt_blocks, *codes, x1)


def kernel(x, ln1_w, w_in, q_norm_w, k_norm_w, sgu_norm_w, sgu_w, sgu_b, mix_norm_w, w_out,
           ln2_w, peer_w_query, peer_keys1, peer_keys2, peer_u, peer_v, rel_bias):
    b, s, d = x.shape
    t = b * s
    bias = _bias_tables(rel_bias)
    for l in range(ln1_w.shape[0]):
        x2 = x.reshape(t, d)
        qkv, sgu_n = _in_proj(x2, ln1_w[l][None], w_in[l].astype(BF16), sgu_norm_w[l], sgu_w[l],
                              sgu_b[l], mix_norm_w[l, D_ATTN:][None], tm=512)
        attn = _attention(qkv.reshape(b, s, 3 * D_ATTN), q_norm_w[l][None], k_norm_w[l][None], bias)
        x1, hnt = _out_proj(attn.reshape(t, D_ATTN), sgu_n, x2, mix_norm_w[l, :D_ATTN][None],
                            w_out[l].astype(BF16), ln2_w[l][None], tm=512)
        codes = _route(hnt, peer_w_query[l].T.astype(BF16), peer_keys1[l].astype(BF16),
                       peer_keys2[l].astype(BF16), tb=256)
        vt_blocks = peer_v[l].reshape(-1, EXPERT_BLOCK, d).transpose(0, 2, 1).astype(BF16)
        out = _experts(hnt, peer_u[l].astype(BF16), vt_blocks, codes, x1, tb=EXPERT_TOKENS)
        x = out.reshape(b, s, d)
    return x
```
